```python
import jax, jax.numpy as jnp
from jax import lax
import numpy as np

D_MODEL = 4096
BATCH = 2
SEQ = 4096
DEPTH = 2

MEM_LEN = 256
N_MIXERS = 2
N_A_LAYERS = (DEPTH + 1) // 2
N_B_LAYERS = DEPTH // 2
X_HEADS = 4
X_WIDTH = D_MODEL // 4
X_HEAD_DIM = X_WIDTH // X_HEADS
MIX_WIDTH = D_MODEL - X_WIDTH
A_HEAD_DIM = 128
A_GROUPS = ((128, 1), (512, 4), (2048, 16))
A_HEADS = MIX_WIDTH // A_HEAD_DIM
A_HEADS_PER_GROUP = A_HEADS // len(A_GROUPS)
A_OUT_WIDTH = A_HEADS_PER_GROUP * A_HEAD_DIM
A_IN_COLS = 3 * MIX_WIDTH + X_WIDTH
BLOCK = 128
ROPE_THETA = 500000.0
ROT_DIM = A_HEAD_DIM // 4
R_HEAD_DIM = 64
R_HEADS = MIX_WIDTH // R_HEAD_DIM
D_DECAY = 128
D_AAA = 128
D_MV = 96
D_GATE = 480
B_SHIFT_COLS = 3 * MIX_WIDTH + D_DECAY + D_AAA + D_MV + D_GATE
B_IN_COLS = B_SHIFT_COLS + X_WIDTH
D_FF = 14336
CONV_WIDTH = 3
NORM_EPS = 1e-6
GN_EPS = 64e-5
L2_EPS = 1e-12
NEG_INF = -1e30

kernel_name = "hybrid_dilated_attn_rwkv7_convffn"


def _split(x, sizes):
    out, start = [], 0
    for s in sizes:
        out.append(x[..., start:start + s])
        start += s
    return out


def rmsnorm(x, g):
    xf = x.astype(jnp.float32)
    y = xf * lax.rsqrt(jnp.mean(xf * xf, axis=-1, keepdims=True) + NORM_EPS)
    return (y * g.astype(jnp.float32)).astype(x.dtype)


def partial_rope(x, positions):
    half = ROT_DIM // 2
    inv_freq = ROPE_THETA ** (-jnp.arange(half, dtype=jnp.float32) / half)
    ang = positions.astype(jnp.float32)[..., None] * inv_freq
    cos, sin = jnp.cos(ang)[:, :, None, :], jnp.sin(ang)[:, :, None, :]
    xr = x[..., :ROT_DIM].astype(jnp.float32)
    x1, x2 = xr[..., :half], xr[..., half:]
    rot = jnp.concatenate([x1 * cos - x2 * sin, x2 * cos + x1 * sin], axis=-1)
    return jnp.concatenate([rot.astype(x.dtype), x[..., ROT_DIM:]], axis=-1)


def dilated_window_attn(q, k, v, window, dilation):
    B, S, H, Dh = q.shape
    n_back = window // dilation
    assert n_back <= BLOCK
    span = dilation * BLOCK
    Sp = -(-S // span) * span
    L = Sp // dilation
    nb = L // BLOCK
    pad = ((0, 0), (0, Sp - S), (0, 0), (0, 0))

    def split(t):
        t = jnp.pad(t, pad).reshape(B, L, dilation, H, Dh).transpose(0, 2, 1, 3, 4)
        return t.reshape(B, dilation, nb, BLOCK, H, Dh)

    def with_prev(t):
        prev = jnp.pad(t, ((0, 0), (0, 0), (1, 0), (0, 0), (0, 0), (0, 0)))[:, :, :-1]
        return jnp.concatenate([prev, t], axis=3)

    qb = split(q)
    kb = with_prev(split(k))
    vb = with_prev(split(v))
    s = jnp.einsum('brnqhd,brnkhd->brnhqk', qb, kb,
                   preferred_element_type=jnp.float32) * (Dh ** -0.5)
    nidx = jnp.arange(nb)[:, None, None]
    qpos = nidx * BLOCK + jnp.arange(BLOCK)[None, :, None]
    kpos = nidx * BLOCK - BLOCK + jnp.arange(2 * BLOCK)[None, None, :]
    dist = qpos - kpos
    valid = (dist >= 0) & (dist <= n_back) & (kpos >= 0)
    s = jnp.where(valid[None, None, :, None, :, :], s, NEG_INF)
    m = jnp.max(s, axis=-1, keepdims=True)
    p = jnp.exp(s - m)
    l = jnp.sum(p, axis=-1, keepdims=True)
    o = jnp.einsum('brnhqk,brnkhd->brnqhd', p / l, vb.astype(jnp.float32))
    lse = (m + jnp.log(l))[..., 0]
    o = o.reshape(B, dilation, L, H, Dh).transpose(0, 2, 1, 3, 4).reshape(B, Sp, H, Dh)[:, :S]
    lse = lse.transpose(0, 1, 2, 4, 3).reshape(B, dilation, L, H).transpose(0, 2, 1, 3)
    lse = lse.reshape(B, Sp, H)[:, :S]
    return o, lse


def dilated_mixer(h, w_in, positions):
    B, S, _ = h.shape
    P = h @ w_in
    q, k, v, cq = _split(P, [MIX_WIDTH, MIX_WIDTH, MIX_WIDTH, X_WIDTH])
    qh = partial_rope(q.reshape(B, S, A_HEADS, A_HEAD_DIM), positions)
    kh = partial_rope(k.reshape(B, S, A_HEADS, A_HEAD_DIM), positions)
    vh = v.reshape(B, S, A_HEADS, A_HEAD_DIM)
    outs, lses = [], []
    for g, (window, dilation) in enumerate(A_GROUPS):
        sl = slice(g * A_HEADS_PER_GROUP, (g + 1) * A_HEADS_PER_GROUP)
        o, lse = dilated_window_attn(qh[:, :, sl], kh[:, :, sl], vh[:, :, sl], window, dilation)
        outs.append(o)
        lses.append(lse)
    wts = jax.nn.softmax(jnp.stack(lses, axis=0), axis=0)
    mix = jnp.sum(wts[..., None] * jnp.stack(outs, axis=0), axis=0)
    return mix.reshape(B, S, A_OUT_WIDTH).astype(h.dtype), cq, v


def wkv7_scan(r, w, k, v, a, b):
    B, S, H, N = r.shape

    def step(state, inp):
        r_t, w_t, k_t, v_t, a_t, b_t = inp
        sa = jnp.einsum('bhij,bhj->bhi', state, a_t)
        state = (state * w_t[:, :, None, :] + sa[..., None] * b_t[:, :, None, :]
                 + v_t[..., None] * k_t[:, :, None, :])
        return state, jnp.einsum('bhij,bhj->bhi', state, r_t)

    xs = tuple(t.transpose(1, 0, 2, 3) for t in (r, w, k, v, a, b))
    state0 = jnp.zeros((B, H, N, N), jnp.float32)
    _, ys = lax.scan(step, state0, xs)
    return ys.transpose(1, 0, 2, 3)


def rwkv7_mixer(h, v_first, w_in, shift_mu, w0, w_up, a0, a_up, v0, v_up, g_up,
                k_k, k_a, r_k, ln_g, ln_b):
    B, S, _ = h.shape
    f32 = jnp.float32
    P = h @ w_in
    Ps, cq = P[..., :B_SHIFT_COLS], P[..., B_SHIFT_COLS:]
    prev = jnp.pad(Ps, ((0, 0), (1, 0), (0, 0)))[:, :-1]
    Ps = Ps + (prev - Ps) * shift_mu
    r, k, v, wd, ad, vd, gd = _split(Ps, [MIX_WIDTH, MIX_WIDTH, MIX_WIDTH,
                                          D_DECAY, D_AAA, D_MV, D_GATE])
    w = -jax.nn.softplus(-(w0 + jnp.tanh(wd) @ w_up).astype(f32)) - 0.5
    decay = jnp.exp(-jnp.exp(w))
    a = jax.nn.sigmoid((a0 + ad @ a_up).astype(f32))
    v = v.astype(f32)
    v = v + (v_first.astype(f32) - v) * jax.nn.sigmoid((v0 + vd @ v_up).astype(f32))
    g = (jax.nn.sigmoid(gd) @ g_up).astype(f32)
    k = k.astype(f32)
    hs = lambda t: t.reshape(B, S, R_HEADS, R_HEAD_DIM)
    kk = hs(k * k_k.astype(f32))
    kk = kk / jnp.maximum(jnp.linalg.norm(kk, axis=-1, keepdims=True), L2_EPS)
    k = k * (1.0 + (a - 1.0) * k_a.astype(f32))
    rh, kh, vh, ah = hs(r.astype(f32)), hs(k), hs(v), hs(a)
    y = wkv7_scan(rh, hs(decay), kh, vh, -kk, kk * ah)
    mu = jnp.mean(y, axis=-1, keepdims=True)
    var = jnp.mean(jnp.square(y - mu), axis=-1, keepdims=True)
    y = ((y - mu) * lax.rsqrt(var + GN_EPS) * ln_g.astype(f32).reshape(R_HEADS, R_HEAD_DIM)
         + ln_b.astype(f32).reshape(R_HEADS, R_HEAD_DIM))
    y = y + jnp.sum(rh * kh * r_k.astype(f32), axis=-1, keepdims=True) * vh
    return (y.reshape(B, S, MIX_WIDTH) * g).astype(h.dtype), cq


def memory_cross_attn(cq, mem_k, mem_v):
    B, S, _ = cq.shape
    q = cq.reshape(B, S, X_HEADS, X_HEAD_DIM)
    s = jnp.einsum('bshd,bmhd->bhsm', q, mem_k,
                   preferred_element_type=jnp.float32) * (X_HEAD_DIM ** -0.5)
    p = jax.nn.softmax(s, axis=-1)
    o = jnp.einsum('bhsm,bmhd->bshd', p, mem_v.astype(jnp.float32))
    return o.reshape(B, S, X_WIDTH).astype(cq.dtype)


def conv_ffn(h, w_up, conv_w, conv_b, w_down):
    S = h.shape[1]
    gate, up = _split(h @ w_up, [D_FF, D_FF])
    gp = jnp.pad(gate, ((0, 0), (CONV_WIDTH - 1, 0), (0, 0)))
    gate = conv_b + sum(conv_w[j] * gp[:, j:j + S] for j in range(CONV_WIDTH))
    return (jax.nn.silu(gate) * up) @ w_down


def setup_inputs(seed: int = 0) -> dict:
    key = jax.random.key(seed)
    ks = jax.random.split(key, 32)
    f32 = jnp.float32
    nrm = lambda i, shape, scale: jax.random.normal(ks[i], shape, f32) * scale
    gain = lambda i, shape: 1.0 + nrm(i, shape, 0.02)
    NA, NB = N_A_LAYERS, N_B_LAYERS
    offset = jax.random.randint(ks[2], (BATCH, 1), 0, 4096, dtype=jnp.int32)
    positions = offset + jnp.arange(SEQ, dtype=jnp.int32)[None, :]
    return {
        "x": nrm(0, (BATCH, SEQ, D_MODEL), 1.0),
        "mem": nrm(1, (BATCH, MEM_LEN, D_MODEL), 1.0),
        "positions": positions,
        "mem_norm_g": gain(3, (D_MODEL,)),
        "mem_w_kv": nrm(4, (D_MODEL, 2 * X_WIDTH), D_MODEL ** -0.5),
        "a_norm_g": gain(5, (NA, D_MODEL)),
        "a_w_in": nrm(6, (NA, D_MODEL, A_IN_COLS), D_MODEL ** -0.5),
        "a_w_out": nrm(7, (NA, A_OUT_WIDTH + X_WIDTH, D_MODEL), (A_OUT_WIDTH + X_WIDTH) ** -0.5),
        "b_norm_g": gain(8, (NB, D_MODEL)),
        "b_w_in": nrm(9, (NB, D_MODEL, B_IN_COLS), D_MODEL ** -0.5),
        "b_shift_mu": jax.random.uniform(ks[10], (NB, B_SHIFT_COLS), f32),
        "b_w0": jax.random.uniform(ks[11], (NB, MIX_WIDTH), f32, minval=-6.0, maxval=-1.0),
        "b_w_up": nrm(12, (NB, D_DECAY, MIX_WIDTH), 0.5 * D_DECAY ** -0.5),
        "b_a0": nrm(13, (NB, MIX_WIDTH), 0.1),
        "b_a_up": nrm(14, (NB, D_AAA, MIX_WIDTH), D_AAA ** -0.5),
        "b_v0": nrm(15, (NB, MIX_WIDTH), 0.1),
        "b_v_up": nrm(16, (NB, D_MV, MIX_WIDTH), D_MV ** -0.5),
        "b_g_up": nrm(17, (NB, D_GATE, MIX_WIDTH), D_GATE ** -0.5),
        "b_k_k": 0.85 + nrm(18, (NB, MIX_WIDTH), 0.05),
        "b_k_a": 1.0 + nrm(19, (NB, MIX_WIDTH), 0.05),
        "b_r_k": nrm(20, (NB, R_HEADS, R_HEAD_DIM), 0.1),
        "b_ln_g": gain(21, (NB, MIX_WIDTH)),
        "b_ln_b": nrm(22, (NB, MIX_WIDTH), 0.02),
        "b_w_out": nrm(23, (NB, MIX_WIDTH + X_WIDTH, D_MODEL), (MIX_WIDTH + X_WIDTH) ** -0.5),
        "ffn_norm_g": gain(24, (DEPTH, D_MODEL)),
        "ffn_w_up": nrm(25, (DEPTH, D_MODEL, 2 * D_FF), D_MODEL ** -0.5),
        "ffn_conv_w": nrm(26, (DEPTH, CONV_WIDTH, D_FF), CONV_WIDTH ** -0.5),
        "ffn_conv_b": nrm(27, (DEPTH, D_FF), 0.02),
        "ffn_w_down": nrm(28, (DEPTH, D_FF, D_MODEL), D_FF ** -0.5),
        "final_norm_g": gain(29, (D_MODEL,)),
    }


def reference(x, mem, positions, mem_norm_g, mem_w_kv, a_norm_g, a_w_in, a_w_out,
              b_norm_g, b_w_in, b_shift_mu, b_w0, b_w_up, b_a0, b_a_up, b_v0, b_v_up,
              b_g_up, b_k_k, b_k_a, b_r_k, b_ln_g, b_ln_b, b_w_out,
              ffn_norm_g, ffn_w_up, ffn_conv_w, ffn_conv_b, ffn_w_down, final_norm_g):
    B, M, _ = mem.shape
    mem_k, mem_v = _split(rmsnorm(mem, mem_norm_g) @ mem_w_kv, [X_WIDTH, X_WIDTH])
    mem_k = mem_k.reshape(B, M, X_HEADS, X_HEAD_DIM)
    mem_v = mem_v.reshape(B, M, X_HEADS, X_HEAD_DIM)
    v_first = None
    for i in range(DEPTH):
        j = i // N_MIXERS
        if i % N_MIXERS == 0:
            h = rmsnorm(x, a_norm_g[j])
            mix, cq, v_attn = dilated_mixer(h, a_w_in[j], positions)
            if v_first is None:
                v_first = v_attn
            w_out = a_w_out[j]
        else:
            h = rmsnorm(x, b_norm_g[j])
            mix, cq = rwkv7_mixer(h, v_first, b_w_in[j], b_shift_mu[j], b_w0[j], b_w_up[j],
                                  b_a0[j], b_a_up[j], b_v0[j], b_v_up[j], b_g_up[j],
                                  b_k_k[j], b_k_a[j], b_r_k[j], b_ln_g[j], b_ln_b[j])
            w_out = b_w_out[j]
        xo = memory_cross_attn(cq, mem_k, mem_v)
        x = x + jnp.concatenate([mix, xo], axis=-1) @ w_out
        x = x + conv_ffn(rmsnorm(x, ffn_norm_g[i]), ffn_w_up[i], ffn_conv_w[i],
                         ffn_conv_b[i], ffn_w_down[i])
    return rmsnorm(x, final_norm_g)
```

```python
import functools
import math

import jax
import jax.numpy as jnp
from jax import lax
from jax.experimental import pallas as pl
from jax.experimental.pallas import tpu as pltpu

f32 = jnp.float32
bf16 = jnp.bfloat16

X_HEADS = 4
A_HEAD_DIM = 128
A_GROUPS = ((128, 1), (512, 4), (2048, 16))
ATT_BLOCK = 128
ROT_DIM = A_HEAD_DIM // 4
ROPE_THETA = 500000.0
R_HEAD_DIM = 64
CONV_WIDTH = 3
NORM_EPS = 1e-6
GN_EPS = 64e-5
L2_EPS = 1e-12
NEG_INF = -1e30

V7X_VMEM_BYTES = 64 * 2**20
VMEM_LIMIT_BYTES = V7X_VMEM_BYTES - 8 * 2**20
SUBLANES = 8
LANES = 128

SCAN_CHUNK = LANES // 2
HEADS_PER_LANE_GROUP = LANES // R_HEAD_DIM


def _pick_tile(n, preferred):
    if n % preferred == 0:
        return preferred
    g = math.gcd(n, preferred)
    return g if g % LANES == 0 else n


def _params(semantics):
    return pltpu.CompilerParams(dimension_semantics=semantics,
                                vmem_limit_bytes=VMEM_LIMIT_BYTES)


def _dot(a, b):
    return jnp.dot(a.astype(bf16), b.astype(bf16), preferred_element_type=f32)


def _dot_nt(a, b):
    return lax.dot_general(a.astype(bf16), b.astype(bf16), (((1,), (1,)), ((), ())),
                           preferred_element_type=f32)


def _dot_tn(a, b):
    return lax.dot_general(a.astype(bf16), b.astype(bf16), (((0,), (0,)), ((), ())),
                           preferred_element_type=f32)


def _sigmoid(x):
    return 1.0 / (1.0 + jnp.exp(-x))


def _split_dot(x, ones_mat):
    hi = x.astype(bf16)
    lo = (x - hi.astype(f32)).astype(bf16)
    return (jnp.dot(hi, ones_mat, preferred_element_type=f32)
            + jnp.dot(lo, ones_mat, preferred_element_type=f32))


def _head_block_ones(width, head_dim):
    shift = int(math.log2(head_dim))
    r = lax.broadcasted_iota(jnp.int32, (width, width), 0) >> shift
    c = lax.broadcasted_iota(jnp.int32, (width, width), 1) >> shift
    return jnp.where(r == c, 1.0, 0.0).astype(bf16)


def _rmsnorm_kernel(x_ref, g_ref, o_ref):
    x = x_ref[...].astype(f32)
    ms = jnp.mean(x * x, axis=-1, keepdims=True)
    o_ref[...] = (x * lax.rsqrt(ms + NORM_EPS) * g_ref[...]).astype(o_ref.dtype)


def rmsnorm(x, g, out_dtype, tm=256):
    m, d = x.shape
    tm = min(tm, m)
    return pl.pallas_call(
        _rmsnorm_kernel,
        out_shape=jax.ShapeDtypeStruct((m, d), out_dtype),
        grid=(m // tm,),
        in_specs=[pl.BlockSpec((tm, d), lambda i: (i, 0)),
                  pl.BlockSpec((1, d), lambda i: (0, 0))],
        out_specs=pl.BlockSpec((tm, d), lambda i: (i, 0)),
        compiler_params=_params(("parallel",)),
        name="rmsnorm",
    )(x, g.reshape(1, d))


def _matmul_kernel(*refs, nk, has_res):
    if has_res:
        a_ref, w_ref, res_ref, o_ref = refs[:4]
        scratch = refs[4:]
    else:
        a_ref, w_ref, o_ref = refs[:3]
        res_ref = None
        scratch = refs[3:]
    part = _dot(a_ref[...], w_ref[...])

    def finish(acc):
        if has_res:
            acc = acc + res_ref[...]
        o_ref[...] = acc.astype(o_ref.dtype)

    if nk == 1:
        finish(part)
        return
    acc_ref, = scratch
    k = pl.program_id(2)

    @pl.when(k == 0)
    def _():
        acc_ref[...] = part

    @pl.when(k > 0)
    def _():
        acc_ref[...] += part

    @pl.when(k == nk - 1)
    def _():
        finish(acc_ref[...])


def matmul(a, w, layer, *, n_cols=None, res=None, out_dtype=f32, tm=1024, tn=512, tk=None):
    m, k_dim = a.shape
    n = n_cols if n_cols is not None else w.shape[2]
    tm, tn = _pick_tile(m, tm), _pick_tile(n, tn)
    tk = k_dim if tk is None else _pick_tile(k_dim, tk)
    assert m % tm == 0 and n % tn == 0 and k_dim % tk == 0, (a.shape, w.shape, tm, tn, tk)
    nk = k_dim // tk
    in_specs = [pl.BlockSpec((tm, tk), lambda i, j, k: (i, k)),
                pl.BlockSpec((None, tk, tn), lambda i, j, k: (layer, k, j))]
    args = [a, w]
    if res is not None:
        in_specs.append(pl.BlockSpec((tm, tn), lambda i, j, k: (i, j)))
        args.append(res)
    return pl.pallas_call(
        functools.partial(_matmul_kernel, nk=nk, has_res=res is not None),
        out_shape=jax.ShapeDtypeStruct((m, n), out_dtype),
        grid=(m // tm, n // tn, nk),
        in_specs=in_specs,
        out_specs=pl.BlockSpec((tm, tn), lambda i, j, k: (i, j)),
        scratch_shapes=[pltpu.VMEM((tm, tn), f32)] if nk > 1 else [],
        compiler_params=_params(("parallel", "parallel", "arbitrary")),
        name="matmul",
    )(*args)


def _ffn_up_kernel(a_ref, wg_ref, wu_ref, cw_ref, cb_ref, o_ref, wg_bf, wu_bf, tail_ref,
                   *, tiles_per_seq):
    i = pl.program_id(1)

    @pl.when(i == 0)
    def _():
        wg_bf[...] = wg_ref[...].astype(bf16)
        wu_bf[...] = wu_ref[...].astype(bf16)

    a = a_ref[...]
    gate = jnp.dot(a, wg_bf[...], preferred_element_type=f32)
    up = jnp.dot(a, wu_bf[...], preferred_element_type=f32)
    tm = gate.shape[0]

    tail = jnp.where(i % tiles_per_seq == 0, 0.0, tail_ref[...])
    tail_ref[...] = gate[tm - SUBLANES:, :]

    row = lax.broadcasted_iota(jnp.int32, gate.shape, 0)
    last, last2 = tail[SUBLANES - 1:SUBLANES, :], tail[SUBLANES - 2:SUBLANES - 1, :]
    g1 = jnp.where(row == 0, last, pltpu.roll(gate, 1, 0))
    g2 = jnp.where(row == 0, last2, jnp.where(row == 1, last, pltpu.roll(gate, 2, 0)))
    cw = cw_ref[...]
    conv = cb_ref[...] + cw[0:1, :] * g2 + cw[1:2, :] * g1 + cw[2:3, :] * gate
    o_ref[...] = (conv * _sigmoid(conv) * up).astype(o_ref.dtype)


def ffn_up(h, w_up, conv_w, conv_b, layer, seq_len, *, tm=512, tn=512):
    m, d = h.shape
    d_ff = w_up.shape[2] // 2
    tm, tn = _pick_tile(seq_len, tm), _pick_tile(d_ff, tn)
    assert m % tm == 0 and seq_len % tm == 0 and d_ff % tn == 0
    nj = d_ff // tn
    depth = conv_w.shape[0]
    return pl.pallas_call(
        functools.partial(_ffn_up_kernel, tiles_per_seq=seq_len // tm),
        out_shape=jax.ShapeDtypeStruct((m, d_ff), bf16),
        grid=(nj, m // tm),
        in_specs=[pl.BlockSpec((tm, d), lambda j, i: (i, 0)),
                  pl.BlockSpec((None, d, tn), lambda j, i: (layer, 0, j)),
                  pl.BlockSpec((None, d, tn), lambda j, i: (layer, 0, j + nj)),
                  pl.BlockSpec((None, CONV_WIDTH, tn), lambda j, i: (layer, 0, j)),
                  pl.BlockSpec((None, 1, tn), lambda j, i: (layer, 0, j))],
        out_specs=pl.BlockSpec((tm, tn), lambda j, i: (i, j)),
        scratch_shapes=[pltpu.VMEM((d, tn), bf16), pltpu.VMEM((d, tn), bf16),
                        pltpu.VMEM((SUBLANES, tn), f32)],
        compiler_params=_params(("arbitrary", "arbitrary")),
        name="ffn_up",
    )(h, w_up, w_up, conv_w, conv_b.reshape(depth, 1, d_ff))


def _rope_table_kernel(pos_ref, freq_ref, o_ref):
    ang = pos_ref[...].astype(f32) * freq_ref[...]
    cos, sin = jnp.cos(ang), jnp.sin(ang)
    lane = lax.broadcasted_iota(jnp.int32, ang.shape, 1)
    half = ROT_DIM // 2
    o_ref[:, 0:LANES] = jnp.where(lane < ROT_DIM, cos, 1.0)
    o_ref[:, LANES:2 * LANES] = jnp.where((lane >= half) & (lane < ROT_DIM), sin, 0.0)
    o_ref[:, 2 * LANES:3 * LANES] = jnp.where(lane < half, -sin, 0.0)


def rope_tables(positions, tm=512):
    t = positions.size
    half = ROT_DIM // 2
    inv_freq = ROPE_THETA ** (-jnp.arange(half, dtype=f32) / half)
    freq = jnp.zeros((1, A_HEAD_DIM), f32).at[0, :ROT_DIM].set(jnp.tile(inv_freq, 2))
    tm = min(tm, t)
    return pl.pallas_call(
        _rope_table_kernel,
        out_shape=jax.ShapeDtypeStruct((t, 3 * LANES), f32),
        grid=(t // tm,),
        in_specs=[pl.BlockSpec((tm, 1), lambda i: (i, 0)),
                  pl.BlockSpec((1, A_HEAD_DIM), lambda i: (0, 0))],
        out_specs=pl.BlockSpec((tm, 3 * LANES), lambda i: (i, 0)),
        compiler_params=_params(("parallel",)),
        name="rope_tables",
    )(positions.reshape(t, 1), freq)


def _rope(x, tab):
    half = ROT_DIM // 2
    c, sa, sb = tab[:, 0:LANES], tab[:, LANES:2 * LANES], tab[:, 2 * LANES:3 * LANES]
    return x * c + pltpu.roll(x, half, 1) * sa + pltpu.roll(x, A_HEAD_DIM - half, 1) * sb


def _attn_group_kernel(q_ref, kc_ref, kp_ref, vc_ref, vp_ref, tc_ref, tp_ref, o_ref, lse_ref,
                       *, n_back, heads):
    n = pl.program_id(2)
    blk = ATT_BLOCK
    tab_c, tab_p = tc_ref[...], tp_ref[...]
    qi = lax.broadcasted_iota(jnp.int32, (blk, 2 * blk), 0)
    col = lax.broadcasted_iota(jnp.int32, (blk, 2 * blk), 1)
    dist = qi + blk - col
    first_col = jnp.where(n > 0, 0, blk)
    valid = (dist >= 0) & (dist <= n_back) & (col >= first_col)
    scale = A_HEAD_DIM ** -0.5
    for h in range(heads):
        sl = slice(h * A_HEAD_DIM, (h + 1) * A_HEAD_DIM)
        q = _rope(q_ref[:, sl], tab_c)
        k2 = jnp.concatenate([_rope(kp_ref[:, sl], tab_p), _rope(kc_ref[:, sl], tab_c)], axis=0)
        v2 = jnp.concatenate([vp_ref[:, sl], vc_ref[:, sl]], axis=0)
        s = _dot_nt(q, k2) * scale
        s = jnp.where(valid, s, NEG_INF)
        m = jnp.max(s, axis=-1, keepdims=True)
        p = jnp.exp(s - m)
        l = jnp.sum(p, axis=-1, keepdims=True)
        o_ref[:, sl] = _dot(p / l, v2)
        lse_ref[:, sl] = jnp.broadcast_to(m + jnp.log(l), (blk, A_HEAD_DIM))


def attn_group(p_in, tabs, batch, seq_len, group, window, dilation):
    ncol = p_in.shape[1]
    gw = ncol // 10
    heads = gw // A_HEAD_DIM
    d = dilation
    n_back = window // d
    assert n_back <= ATT_BLOCK and seq_len % (d * ATT_BLOCK) == 0
    nb = seq_len // (d * ATT_BLOCK)
    pv = p_in.reshape(batch, seq_len // d, d * ncol)
    tv = tabs.reshape(batch, seq_len // d, d * 3 * LANES)
    cpb = ncol // gw
    qcol, kcol, vcol = group, 3 + group, 6 + group

    def cur(cb):
        return pl.BlockSpec((None, ATT_BLOCK, gw), lambda b, r, n: (b, n, r * cpb + cb))

    def prev(cb):
        return pl.BlockSpec((None, ATT_BLOCK, gw),
                            lambda b, r, n: (b, jnp.maximum(n - 1, 0), r * cpb + cb))

    tab_cur = pl.BlockSpec((None, ATT_BLOCK, 3 * LANES), lambda b, r, n: (b, n, r))
    tab_prev = pl.BlockSpec((None, ATT_BLOCK, 3 * LANES),
                            lambda b, r, n: (b, jnp.maximum(n - 1, 0), r))
    out_spec = pl.BlockSpec((None, ATT_BLOCK, gw), lambda b, r, n: (b, n, r))
    out_sds = jax.ShapeDtypeStruct((batch, seq_len // d, d * gw), f32)
    o, lse = pl.pallas_call(
        functools.partial(_attn_group_kernel, n_back=n_back, heads=heads),
        out_shape=(out_sds, out_sds),
        grid=(batch, d, nb),
        in_specs=[cur(qcol), cur(kcol), prev(kcol), cur(vcol), prev(vcol), tab_cur, tab_prev],
        out_specs=(out_spec, out_spec),
        compiler_params=_params(("parallel", "parallel", "arbitrary")),
        name=f"attn_group_d{d}",
    )(pv, pv, pv, pv, pv, tv, tv)
    return o.reshape(batch * seq_len, gw), lse.reshape(batch * seq_len, gw)


def _attn_merge_kernel(o0, o1, o2, l0, l1, l2, mix_ref):
    a, b, c = l0[...], l1[...], l2[...]
    m = jnp.maximum(jnp.maximum(a, b), c)
    wa, wb, wc = jnp.exp(a - m), jnp.exp(b - m), jnp.exp(c - m)
    mix = (wa * o0[...] + wb * o1[...] + wc * o2[...]) / (wa + wb + wc)
    mix_ref[...] = mix.astype(mix_ref.dtype)


def attn_merge(outs, lses, tm=256):
    t, gw = outs[0].shape
    tm = min(tm, t)
    spec = pl.BlockSpec((tm, gw), lambda i: (i, 0))
    return pl.pallas_call(
        _attn_merge_kernel,
        out_shape=jax.ShapeDtypeStruct((t, gw), bf16),
        grid=(t // tm,),
        in_specs=[spec] * 6,
        out_specs=spec,
        compiler_params=_params(("parallel",)),
        name="attn_merge",
    )(*outs, *lses)


def _xattn_kernel(q_ref, k_ref, v_ref, o_ref, *, heads):
    hd = q_ref.shape[1] // heads
    scale = hd ** -0.5
    for h in range(heads):
        sl = slice(h * hd, (h + 1) * hd)
        s = _dot_nt(q_ref[:, sl], k_ref[:, sl]) * scale
        m = jnp.max(s, axis=-1, keepdims=True)
        p = jnp.exp(s - m)
        l = jnp.sum(p, axis=-1, keepdims=True)
        o_ref[:, sl] = _dot(p / l, v_ref[:, sl]).astype(o_ref.dtype)


def cross_attn(q_arr, q_col_block, mem_kv, batch, seq_len, tq=512):
    xw = mem_kv.shape[1] // 2
    mem_len = mem_kv.shape[0] // batch
    tq = min(tq, seq_len)
    nq = seq_len // tq
    return pl.pallas_call(
        functools.partial(_xattn_kernel, heads=X_HEADS),
        out_shape=jax.ShapeDtypeStruct((batch * seq_len, xw), bf16),
        grid=(batch, nq),
        in_specs=[pl.BlockSpec((tq, xw), lambda b, i: (b * nq + i, q_col_block)),
                  pl.BlockSpec((mem_len, xw), lambda b, i: (b, 0)),
                  pl.BlockSpec((mem_len, xw), lambda b, i: (b, 1))],
        out_specs=pl.BlockSpec((tq, xw), lambda b, i: (b * nq + i, 0)),
        compiler_params=_params(("parallel", "parallel")),
        name="cross_attn",
    )(q_arr, mem_kv, mem_kv)


def _shift(x, halo, mu, seq_start):
    row = lax.broadcasted_iota(jnp.int32, x.shape, 0)
    last = jnp.where(seq_start, 0.0, halo[SUBLANES - 1:SUBLANES, :])
    prev = jnp.where(row == 0, last, pltpu.roll(x, 1, 0))
    return x + (prev - x) * mu


def _rwkv_prep_kernel(pr, pk, pv, hr, hk, hv, lo, hlo, vf,
                      mu_r, mu_k, mu_v, mu_l, w0, a0, v0, k_k, k_a, wup, aup, vup, gup,
                      r_o, lw_o, k_o, v_o, kk_o, b_o, g_o, *, tiles_per_seq):
    seq_start = pl.program_id(1) % tiles_per_seq == 0
    r = _shift(pr[...], hr[...], mu_r[...], seq_start)
    k = _shift(pk[...], hk[...], mu_k[...], seq_start)
    v = _shift(pv[...], hv[...], mu_v[...], seq_start)
    low = _shift(lo[...], hlo[...], mu_l[...], seq_start)

    xw = w0[...] + _dot(jnp.tanh(low), wup[...])
    lw_o[...] = -math.exp(-0.5) * _sigmoid(xw)
    ag = _sigmoid(a0[...] + _dot(low, aup[...]))
    v_gate = _sigmoid(v0[...] + _dot(low, vup[...]))
    g_o[...] = _dot(_sigmoid(low), gup[...])
    v_o[...] = v + (vf[...] - v) * v_gate
    r_o[...] = r

    kk = k * k_k[...]
    ones = _head_block_ones(kk.shape[1], R_HEAD_DIM)
    norm = jnp.sqrt(_split_dot(kk * kk, ones))
    kk = kk / jnp.maximum(norm, L2_EPS)
    kk_o[...] = kk
    b_o[...] = kk * ag
    k_o[...] = k * (1.0 + (ag - 1.0) * k_a[...])


def rwkv_prep(p_rkv, p_low, p_a, mix_w, seq_len, mu, w0, a0, v0, k_k, k_a,
              wup_pad, aup_pad, vup_pad, gup_pad, *, tm=256, tc=512):
    t = p_rkv.shape[0]
    lw_dim = p_low.shape[1]
    tm, tc = _pick_tile(seq_len, tm), _pick_tile(mix_w, tc)
    assert seq_len % tm == 0 and mix_w % tc == 0 and tc % LANES == 0
    nc = mix_w // tc
    hb = tm // SUBLANES

    def tile(cb):
        return pl.BlockSpec((tm, tc), lambda c, i: (i, cb * nc + c))

    def halo(cb):
        return pl.BlockSpec((SUBLANES, tc), lambda c, i: (jnp.maximum(i * hb - 1, 0), cb * nc + c))

    def row(cb=0):
        return pl.BlockSpec((1, tc), lambda c, i: (0, cb * nc + c))

    low_w = pl.BlockSpec((lw_dim, tc), lambda c, i: (0, c))
    out_spec = pl.BlockSpec((tm, tc), lambda c, i: (i, c))
    out_sds = jax.ShapeDtypeStruct((t, mix_w), f32)
    mu_full = mu.reshape(1, -1)
    mu_low = mu[3 * mix_w:].reshape(1, lw_dim)
    in_specs = [tile(0), tile(1), tile(2), halo(0), halo(1), halo(2),
                pl.BlockSpec((tm, lw_dim), lambda c, i: (i, 0)),
                pl.BlockSpec((SUBLANES, lw_dim), lambda c, i: (jnp.maximum(i * hb - 1, 0), 0)),
                tile(2),
                row(0), row(1), row(2),
                pl.BlockSpec((1, lw_dim), lambda c, i: (0, 0)),
                row(), row(), row(), row(), row(),
                low_w, low_w, low_w, low_w]
    return pl.pallas_call(
        functools.partial(_rwkv_prep_kernel, tiles_per_seq=seq_len // tm),
        out_shape=(out_sds,) * 7,
        grid=(nc, t // tm),
        in_specs=in_specs,
        out_specs=(out_spec,) * 7,
        compiler_params=_params(("parallel", "parallel")),
        name="rwkv_prep",
    )(p_rkv, p_rkv, p_rkv, p_rkv, p_rkv, p_rkv, p_low, p_low, p_a,
      mu_full, mu_full, mu_full, mu_low,
      w0.reshape(1, -1), a0.reshape(1, -1), v0.reshape(1, -1), k_k.reshape(1, -1),
      k_a.reshape(1, -1), wup_pad, aup_pad, vup_pad, gup_pad)


def _unit_lower_inverse(x):
    n = x.shape[0]
    row = lax.broadcasted_iota(jnp.int32, (n, n), 0)
    col = lax.broadcasted_iota(jnp.int32, (n, n), 1)
    t = jnp.where(row == col, 1.0, 0.0)
    for level in range(int(math.log2(SCAN_CHUNK))):
        rb, cb = row >> level, col >> level
        lower_left = ((rb >> 1) == (cb >> 1)) & ((rb & 1) == 1) & ((cb & 1) == 0)
        x_off = jnp.where(lower_left, x, 0.0)
        t = t + _dot(t, _dot(x_off, t))
    return t


def _scan_pair(r, lcum, lw, k, v, kk, b, state):
    c = SCAN_CHUNK
    lane = lax.broadcasted_iota(jnp.int32, (c, LANES), 1)
    head0 = lane < R_HEAD_DIM

    def stack(x):
        return jnp.concatenate([jnp.where(head0, x, 0.0), jnp.where(head0, 0.0, x)], axis=0)

    l_end = lcum[c - 1:c, :]
    e_in = jnp.exp(lcum)
    e_out = jnp.exp(-lcum)
    e_end = jnp.exp(l_end - lcum)
    a_t = stack(-kk * jnp.exp(lcum - lw))
    r_t = stack(r * e_in)
    b_t = stack(b * e_out)
    k_t = stack(k * e_out)
    b_h = stack(b * e_end)
    k_h = stack(k * e_end)
    v_s = stack(v)

    s1 = _dot_nt(jnp.concatenate([a_t, r_t], axis=0), jnp.concatenate([b_t, k_t], axis=0))
    n = 2 * c
    row = lax.broadcasted_iota(jnp.int32, (n, n), 0) & (c - 1)
    col = lax.broadcasted_iota(jnp.int32, (n, n), 1) & (c - 1)
    strict, incl = row > col, row >= col
    ab = jnp.where(strict, s1[:n, :n], 0.0)
    ak = jnp.where(strict, s1[:n, n:], 0.0)
    rb = jnp.where(incl, s1[n:, :n], 0.0)
    rk = jnp.where(incl, s1[n:, n:], 0.0)

    t_inv = _unit_lower_inverse(ab)
    u = _dot(t_inv, _dot_nt(a_t, state) + _dot(ak, v_s))
    y_s = _dot_nt(r_t, state) + _dot(jnp.concatenate([rb, rk], axis=1),
                                     jnp.concatenate([u, v_s], axis=0))
    y = y_s[:c, :] + y_s[c:, :]
    new_state = state * jnp.exp(l_end) + _dot_tn(jnp.concatenate([u, v_s], axis=0),
                                                jnp.concatenate([b_h, k_h], axis=0))
    return y, new_state


def _rwkv_scan_kernel(r_ref, lw_ref, k_ref, v_ref, kk_ref, b_ref, g_ref, lng_ref, lnb_ref, rk_ref,
                      o_ref, state_ref):
    @pl.when(pl.program_id(2) == 0)
    def _():
        state_ref[...] = jnp.zeros_like(state_ref)

    c = SCAN_CHUNK
    tri_r = lax.broadcasted_iota(jnp.int32, (c, c), 0)
    tri_c = lax.broadcasted_iota(jnp.int32, (c, c), 1)
    tri = jnp.where(tri_r >= tri_c, 1.0, 0.0)
    lw_all = lw_ref[...]
    lcum_all = jnp.dot(tri, lw_all, preferred_element_type=f32, precision=lax.Precision.HIGHEST)
    ones = _head_block_ones(LANES, R_HEAD_DIM)
    inv_n = 1.0 / R_HEAD_DIM
    for p in range(r_ref.shape[1] // LANES):
        sl = slice(p * LANES, (p + 1) * LANES)
        r, k, v = r_ref[:, sl], k_ref[:, sl], v_ref[:, sl]
        y, new_state = _scan_pair(r, lcum_all[:, sl], lw_all[:, sl], k, v, kk_ref[:, sl],
                                  b_ref[:, sl], state_ref[p])
        state_ref[p] = new_state
        mu = _split_dot(y, ones) * inv_n
        yc = y - mu
        var = _split_dot(yc * yc, ones) * inv_n
        y = yc * lax.rsqrt(var + GN_EPS) * lng_ref[:, sl] + lnb_ref[:, sl]
        y = y + _split_dot(r * k * rk_ref[:, sl], ones) * v
        o_ref[:, sl] = (y * g_ref[:, sl]).astype(o_ref.dtype)


def rwkv_scan(r, lw, k, v, kk, b, g, ln_g, ln_b, r_k, batch, seq_len, *, wb=512):
    t, mix_w = r.shape
    wb = _pick_tile(mix_w, wb)
    assert mix_w % wb == 0 and wb % LANES == 0 and seq_len % SCAN_CHUNK == 0
    nchunk = seq_len // SCAN_CHUNK
    tile = pl.BlockSpec((SCAN_CHUNK, wb), lambda bi, w, n: (bi * nchunk + n, w))
    row = pl.BlockSpec((1, wb), lambda bi, w, n: (0, w))
    return pl.pallas_call(
        _rwkv_scan_kernel,
        out_shape=jax.ShapeDtypeStruct((t, mix_w), bf16),
        grid=(batch, mix_w // wb, nchunk),
        in_specs=[tile] * 7 + [row] * 3,
        out_specs=tile,
        scratch_shapes=[pltpu.VMEM((wb // LANES, LANES, LANES), f32)],
        compiler_params=_params(("parallel", "parallel", "arbitrary")),
        name="rwkv_scan",
    )(r, lw, k, v, kk, b, g, ln_g.reshape(1, -1), ln_b.reshape(1, -1), r_k.reshape(1, -1))


def _pad_rows(w, start, total):
    return jnp.zeros((total, w.shape[1]), w.dtype).at[start:start + w.shape[0]].set(w)


def kernel(x, mem, positions, mem_norm_g, mem_w_kv, a_norm_g, a_w_in, a_w_out, b_norm_g, b_w_in, b_shift_mu, b_w0, b_w_up, b_a0, b_a_up, b_v0, b_v_up, b_g_up, b_k_k, b_k_a, b_r_k, b_ln_g, b_ln_b, b_w_out, ffn_norm_g, ffn_w_up, ffn_conv_w, ffn_conv_b, ffn_w_down, final_norm_g):
    batch, seq_len, d_model = x.shape
    t = batch * seq_len
    depth = ffn_w_up.shape[0]
    xw = d_model // 4
    mix_w = d_model - xw
    x = x.reshape(t, d_model)

    mem_h = rmsnorm(mem.reshape(-1, d_model), mem_norm_g, bf16)
    mem_kv = matmul(mem_h, mem_w_kv[None], 0)
    tabs = rope_tables(positions)

    p_a = None
    for i in range(depth):
        j = i // 2
        if i % 2 == 0:
            h = rmsnorm(x, a_norm_g[j], bf16)
            p = matmul(h, a_w_in, j)
            if p_a is None:
                p_a = p
            outs, lses = [], []
            for g, (window, dilation) in enumerate(A_GROUPS):
                o, lse = attn_group(p, tabs, batch, seq_len, g, window, dilation)
                outs.append(o)
                lses.append(lse)
            mix = attn_merge(outs, lses)
            xo = cross_attn(p, 3 * mix_w // xw, mem_kv, batch, seq_len)
            w_out = a_w_out
        else:
            h = rmsnorm(x, b_norm_g[j], bf16)
            low_w = b_w_in.shape[2] - 3 * mix_w - xw
            p_rkv = matmul(h, b_w_in, j, n_cols=3 * mix_w)
            w_tail = b_w_in[j][:, 3 * mix_w:]
            p_low = matmul(h, w_tail[None, :, :low_w], 0)
            cq = matmul(h, w_tail[None, :, low_w:], 0, out_dtype=bf16)
            dd, da, dv = b_w_up.shape[1], b_a_up.shape[1], b_v_up.shape[1]
            prep = rwkv_prep(
                p_rkv, p_low, p_a, mix_w, seq_len, b_shift_mu[j], b_w0[j], b_a0[j], b_v0[j],
                b_k_k[j], b_k_a[j],
                _pad_rows(b_w_up[j], 0, low_w), _pad_rows(b_a_up[j], dd, low_w),
                _pad_rows(b_v_up[j], dd + da, low_w), _pad_rows(b_g_up[j], dd + da + dv, low_w))
            mix = rwkv_scan(*prep, b_ln_g[j], b_ln_b[j], b_r_k[j], batch, seq_len)
            xo = cross_attn(cq, 0, mem_kv, batch, seq_len)
            w_out = b_w_out
        x = matmul(jnp.concatenate([mix, xo], axis=-1), w_out, j, res=x)
        h = rmsnorm(x, ffn_norm_g[i], bf16)
        act = ffn_up(h, ffn_w_up, ffn_conv_w, ffn_conv_b, i, seq_len)
        x = matmul(act, ffn_w_down, i, res=x, tn=1024, tk=1024)
    return rmsnorm(x, final_norm_g, f32).reshape(batch, seq_len, d_model)
```

```python
import functools
import math

import jax
import jax.numpy as jnp
from jax import lax
from jax.experimental import pallas as pl
from jax.experimental.pallas import tpu as pltpu

f32 = jnp.float32
bf16 = jnp.bfloat16

X_HEADS = 4
A_HEAD_DIM = 128
A_GROUPS = ((128, 1), (512, 4), (2048, 16))
ATT_BLOCK = 128
ROT_DIM = A_HEAD_DIM // 4
ROPE_THETA = 500000.0
R_HEAD_DIM = 64
CONV_WIDTH = 3
NORM_EPS = 1e-6
GN_EPS = 64e-5
L2_EPS = 1e-12
NEG_INF = -1e30

V7X_VMEM_BYTES = 64 * 2**20
VMEM_LIMIT_BYTES = V7X_VMEM_BYTES - 8 * 2**20
SUBLANES = 8
LANES = 128

SCAN_CHUNK = LANES // 2
HEADS_PER_LANE_GROUP = LANES // R_HEAD_DIM


def _pick_tile(n, preferred):
    if n % preferred == 0:
        return preferred
    g = math.gcd(n, preferred)
    return g if g % LANES == 0 else n


def _params(semantics):
    return pltpu.CompilerParams(dimension_semantics=semantics,
                                vmem_limit_bytes=VMEM_LIMIT_BYTES)


def _dot(a, b):
    return jnp.dot(a.astype(bf16), b.astype(bf16), preferred_element_type=f32)


def _dot_nt(a, b):
    return lax.dot_general(a.astype(bf16), b.astype(bf16), (((1,), (1,)), ((), ())),
                           preferred_element_type=f32)


def _dot_tn(a, b):
    return lax.dot_general(a.astype(bf16), b.astype(bf16), (((0,), (0,)), ((), ())),
                           preferred_element_type=f32)


def _sigmoid(x):
    return 1.0 / (1.0 + jnp.exp(-x))


def _split_dot(x, ones_mat):
    hi = x.astype(bf16)
    lo = (x - hi.astype(f32)).astype(bf16)
    return (jnp.dot(hi, ones_mat, preferred_element_type=f32)
            + jnp.dot(lo, ones_mat, preferred_element_type=f32))


def _head_block_ones(width, head_dim):
    shift = int(math.log2(head_dim))
    r = lax.broadcasted_iota(jnp.int32, (width, width), 0) >> shift
    c = lax.broadcasted_iota(jnp.int32, (width, width), 1) >> shift
    return jnp.where(r == c, 1.0, 0.0).astype(bf16)


def _rmsnorm_kernel(x_ref, g_ref, o_ref):
    x = x_ref[...].astype(f32)
    ms = jnp.mean(x * x, axis=-1, keepdims=True)
    o_ref[...] = (x * lax.rsqrt(ms + NORM_EPS) * g_ref[...]).astype(o_ref.dtype)


def rmsnorm(x, g, out_dtype, tm=256):
    m, d = x.shape
    tm = min(tm, m)
    return pl.pallas_call(
        _rmsnorm_kernel,
        out_shape=jax.ShapeDtypeStruct((m, d), out_dtype),
        grid=(m // tm,),
        in_specs=[pl.BlockSpec((tm, d), lambda i: (i, 0)),
                  pl.BlockSpec((1, d), lambda i: (0, 0))],
        out_specs=pl.BlockSpec((tm, d), lambda i: (i, 0)),
        compiler_params=_params(("parallel",)),
        name="rmsnorm",
    )(x, g.reshape(1, d))


def _matmul_kernel(*refs, nk, has_res):
    if has_res:
        a_ref, w_ref, res_ref, o_ref = refs[:4]
        scratch = refs[4:]
    else:
        a_ref, w_ref, o_ref = refs[:3]
        res_ref = None
        scratch = refs[3:]
    if nk == 1:
        out = _dot(a_ref[...], w_ref[...])
        if has_res:
            out = out + res_ref[...]
        o_ref[...] = out.astype(o_ref.dtype)
        return
    acc_ref, = scratch
    k = pl.program_id(2)

    @pl.when(k == 0)
    def _():
        acc_ref[...] = res_ref[...] if has_res else jnp.zeros_like(acc_ref)

    acc_ref[...] += _dot(a_ref[...], w_ref[...])

    @pl.when(k == nk - 1)
    def _():
        o_ref[...] = acc_ref[...].astype(o_ref.dtype)


def matmul(a, w, layer, *, n_cols=None, res=None, out_dtype=f32, tm=1024, tn=512, tk=None):
    m, k_dim = a.shape
    n = n_cols if n_cols is not None else w.shape[2]
    tm, tn = _pick_tile(m, tm), _pick_tile(n, tn)
    tk = k_dim if tk is None else _pick_tile(k_dim, tk)
    assert m % tm == 0 and n % tn == 0 and k_dim % tk == 0, (a.shape, w.shape, tm, tn, tk)
    nk = k_dim // tk
    in_specs = [pl.BlockSpec((tm, tk), lambda i, j, k: (i, k)),
                pl.BlockSpec((None, tk, tn), lambda i, j, k: (layer, k, j))]
    args = [a, w]
    if res is not None:
        in_specs.append(pl.BlockSpec((tm, tn), lambda i, j, k: (i, j)))
        args.append(res)
    return pl.pallas_call(
        functools.partial(_matmul_kernel, nk=nk, has_res=res is not None),
        out_shape=jax.ShapeDtypeStruct((m, n), out_dtype),
        grid=(m // tm, n // tn, nk),
        in_specs=in_specs,
        out_specs=pl.BlockSpec((tm, tn), lambda i, j, k: (i, j)),
        scratch_shapes=[pltpu.VMEM((tm, tn), f32)] if nk > 1 else [],
        compiler_params=_params(("parallel", "parallel", "arbitrary")),
        name="matmul",
    )(*args)


def _out_proj_kernel(mix_ref, xo_ref, w_ref, res_ref, o_ref, acc_ref, *, nk_mix, nk):
    k = pl.program_id(2)

    @pl.when(k == 0)
    def _():
        acc_ref[...] = res_ref[...]

    @pl.when(k < nk_mix)
    def _():
        acc_ref[...] += _dot(mix_ref[...], w_ref[...])

    @pl.when(k >= nk_mix)
    def _():
        acc_ref[...] += _dot(xo_ref[...], w_ref[...])

    @pl.when(k == nk - 1)
    def _():
        o_ref[...] = acc_ref[...]


def out_proj(mix, xo, w, layer, res, *, tm=1024, tn=1024, tk=1024):
    m, k_mix = mix.shape
    k_xo = xo.shape[1]
    n = w.shape[2]
    tm, tn = _pick_tile(m, tm), _pick_tile(n, tn)
    tk = math.gcd(_pick_tile(k_mix, tk), _pick_tile(k_xo, tk))
    assert k_mix % tk == 0 and k_xo % tk == 0 and w.shape[1] == k_mix + k_xo
    nk_mix, nk = k_mix // tk, (k_mix + k_xo) // tk
    return pl.pallas_call(
        functools.partial(_out_proj_kernel, nk_mix=nk_mix, nk=nk),
        out_shape=jax.ShapeDtypeStruct((m, n), f32),
        grid=(m // tm, n // tn, nk),
        in_specs=[pl.BlockSpec((tm, tk), lambda i, j, k: (i, jnp.minimum(k, nk_mix - 1))),
                  pl.BlockSpec((tm, tk), lambda i, j, k: (i, jnp.maximum(k - nk_mix, 0))),
                  pl.BlockSpec((None, tk, tn), lambda i, j, k: (layer, k, j)),
                  pl.BlockSpec((tm, tn), lambda i, j, k: (i, j))],
        out_specs=pl.BlockSpec((tm, tn), lambda i, j, k: (i, j)),
        scratch_shapes=[pltpu.VMEM((tm, tn), f32)],
        compiler_params=_params(("parallel", "parallel", "arbitrary")),
        name="out_proj",
    )(mix, xo, w, res)


def _ffn_up_kernel(a_ref, wg_ref, wu_ref, cw_ref, cb_ref, o_ref, wg_bf, wu_bf, tail_ref,
                   *, tiles_per_seq):
    i = pl.program_id(1)

    @pl.when(i == 0)
    def _():
        wg_bf[...] = wg_ref[...].astype(bf16)
        wu_bf[...] = wu_ref[...].astype(bf16)

    a = a_ref[...]
    gate = jnp.dot(a, wg_bf[...], preferred_element_type=f32)
    up = jnp.dot(a, wu_bf[...], preferred_element_type=f32)
    tm = gate.shape[0]

    tail = jnp.where(i % tiles_per_seq == 0, 0.0, tail_ref[...])
    tail_ref[...] = gate[tm - SUBLANES:, :]

    row = lax.broadcasted_iota(jnp.int32, gate.shape, 0)
    last, last2 = tail[SUBLANES - 1:SUBLANES, :], tail[SUBLANES - 2:SUBLANES - 1, :]
    g1 = jnp.where(row == 0, last, pltpu.roll(gate, 1, 0))
    g2 = jnp.where(row == 0, last2, jnp.where(row == 1, last, pltpu.roll(gate, 2, 0)))
    cw = cw_ref[...]
    conv = cb_ref[...] + cw[0:1, :] * g2 + cw[1:2, :] * g1 + cw[2:3, :] * gate
    o_ref[...] = (conv * _sigmoid(conv) * up).astype(o_ref.dtype)


def ffn_up(h, w_up, conv_w, conv_b, layer, seq_len, *, tm=512, tn=512):
    m, d = h.shape
    d_ff = w_up.shape[2] // 2
    tm, tn = _pick_tile(seq_len, tm), _pick_tile(d_ff, tn)
    assert m % tm == 0 and seq_len % tm == 0 and d_ff % tn == 0
    nj = d_ff // tn
    depth = conv_w.shape[0]
    return pl.pallas_call(
        functools.partial(_ffn_up_kernel, tiles_per_seq=seq_len // tm),
        out_shape=jax.ShapeDtypeStruct((m, d_ff), bf16),
        grid=(nj, m // tm),
        in_specs=[pl.BlockSpec((tm, d), lambda j, i: (i, 0)),
                  pl.BlockSpec((None, d, tn), lambda j, i: (layer, 0, j)),
                  pl.BlockSpec((None, d, tn), lambda j, i: (layer, 0, j + nj)),
                  pl.BlockSpec((None, CONV_WIDTH, tn), lambda j, i: (layer, 0, j)),
                  pl.BlockSpec((None, 1, tn), lambda j, i: (layer, 0, j))],
        out_specs=pl.BlockSpec((tm, tn), lambda j, i: (i, j)),
        scratch_shapes=[pltpu.VMEM((d, tn), bf16), pltpu.VMEM((d, tn), bf16),
                        pltpu.VMEM((SUBLANES, tn), f32)],
        compiler_params=_params(("arbitrary", "arbitrary")),
        name="ffn_up",
    )(h, w_up, w_up, conv_w, conv_b.reshape(depth, 1, d_ff))


def _rope_table_kernel(pos_ref, freq_ref, o_ref):
    ang = pos_ref[...].astype(f32) * freq_ref[...]
    lane = lax.broadcasted_iota(jnp.int32, ang.shape, 1)
    o_ref[0] = jnp.where(lane < ROT_DIM, jnp.cos(ang), 1.0)
    o_ref[1] = jnp.where(lane < ROT_DIM, jnp.sin(ang), 0.0)


def rope_tables(positions, tm=512):
    t = positions.size
    half = ROT_DIM // 2
    inv_freq = ROPE_THETA ** (-jnp.arange(half, dtype=f32) / half)
    freq = jnp.zeros((1, A_HEAD_DIM), f32).at[0, :ROT_DIM].set(jnp.tile(inv_freq, 2))
    tm = _pick_tile(t, tm)
    return pl.pallas_call(
        _rope_table_kernel,
        out_shape=jax.ShapeDtypeStruct((2, t, LANES), f32),
        grid=(t // tm,),
        in_specs=[pl.BlockSpec((tm, 1), lambda i: (i, 0)),
                  pl.BlockSpec((1, A_HEAD_DIM), lambda i: (0, 0))],
        out_specs=pl.BlockSpec((2, tm, LANES), lambda i: (0, i, 0)),
        compiler_params=_params(("parallel",)),
        name="rope_tables",
    )(positions.reshape(t, 1), freq)


def _rope(x, cos, sin):
    half = ROT_DIM // 2
    lane = lax.broadcasted_iota(jnp.int32, x.shape, 1)
    rot = jnp.where(lane < half, -pltpu.roll(x, A_HEAD_DIM - half, 1), pltpu.roll(x, half, 1))
    return x * cos + rot * sin


def _dilated_attn_kernel(*refs, dilations, n_backs):
    ng = len(dilations)
    q_refs, kc_refs, kp_refs = refs[0:ng], refs[ng:2 * ng], refs[2 * ng:3 * ng]
    vc_refs, vp_refs = refs[3 * ng:4 * ng], refs[4 * ng:5 * ng]
    tc_ref, tp_ref, o_ref = refs[5 * ng:5 * ng + 3]
    o_scr, lse_scr = refs[5 * ng + 3:6 * ng + 3], refs[6 * ng + 3:7 * ng + 3]

    blk = ATT_BLOCK
    span = q_refs[0].shape[0]
    first_span = pl.program_id(1) == 0
    qi = lax.broadcasted_iota(jnp.int32, (blk, 2 * blk), 0)
    col = lax.broadcasted_iota(jnp.int32, (blk, 2 * blk), 1)
    dist = qi + blk - col
    first_col = jnp.where(first_span, blk, 0)
    scale = A_HEAD_DIM ** -0.5

    def rows(start, size, d):
        return pl.ds(start, size, stride=d) if d > 1 else pl.ds(start, size)

    for g, (d, n_back) in enumerate(zip(dilations, n_backs)):
        in_window = (dist >= 0) & (dist <= n_back)
        in_window_first = in_window & (col >= first_col)
        span_g = d * blk
        for r in range(d):
            for j in range(span // span_g):
                cur = rows(j * span_g + r, blk, d)
                if j == 0:
                    prev = rows(r, blk, d)
                    prev_tab = rows(span - span_g + r, blk, d)
                    cos_c, sin_c = tc_ref[0, cur, :], tc_ref[1, cur, :]
                    k2 = jnp.concatenate(
                        [_rope(kp_refs[g][prev, :], tp_ref[0, prev_tab, :], tp_ref[1, prev_tab, :]),
                         _rope(kc_refs[g][cur, :], cos_c, sin_c)], axis=0)
                    v2 = jnp.concatenate([vp_refs[g][prev, :], vc_refs[g][cur, :]], axis=0)
                    valid = in_window_first
                else:
                    both = rows((j - 1) * span_g + r, 2 * blk, d)
                    cos2, sin2 = tc_ref[0, both, :], tc_ref[1, both, :]
                    k2 = _rope(kc_refs[g][both, :], cos2, sin2)
                    v2 = vc_refs[g][both, :]
                    cos_c, sin_c = cos2[blk:, :], sin2[blk:, :]
                    valid = in_window
                q = _rope(q_refs[g][cur, :], cos_c, sin_c)
                s = jnp.where(valid, _dot_nt(q, k2) * scale, NEG_INF)
                m = jnp.max(s, axis=-1, keepdims=True)
                p = jnp.exp(s - m)
                l = jnp.sum(p, axis=-1, keepdims=True)
                o_scr[g][cur, :] = _dot(p / l, v2)
                lse_scr[g][cur, :] = jnp.broadcast_to(m + jnp.log(l), (blk, A_HEAD_DIM))

    lses = [ref[...] for ref in lse_scr]
    m = functools.reduce(jnp.maximum, lses)
    wts = [jnp.exp(x - m) for x in lses]
    num = sum(w * ref[...] for w, ref in zip(wts, o_scr))
    o_ref[...] = (num / sum(wts)).astype(o_ref.dtype)


def dilated_attn(p_in, tabs, batch, seq_len):
    t, ncol = p_in.shape
    ng = len(A_GROUPS)
    gw = ncol // (3 * ng + 1)
    heads = gw // A_HEAD_DIM
    dilations = tuple(d for _, d in A_GROUPS)
    n_backs = tuple(w // d for w, d in A_GROUPS)
    span = max(dilations) * ATT_BLOCK
    assert all(nb <= ATT_BLOCK for nb in n_backs) and seq_len % span == 0
    assert all(span % (d * ATT_BLOCK) == 0 for d in dilations)
    nspan = seq_len // span

    def cur(part, g):
        return pl.BlockSpec((span, A_HEAD_DIM),
                            lambda b, n, h: (b * nspan + n, (part * ng + g) * heads + h))

    def prev(part, g):
        per_span = span // (dilations[g] * ATT_BLOCK)
        return pl.BlockSpec(
            (dilations[g] * ATT_BLOCK, A_HEAD_DIM),
            lambda b, n, h: (jnp.maximum((b * nspan + n) * per_span - 1, 0), (part * ng + g) * heads + h))

    groups = range(ng)
    in_specs = ([cur(0, g) for g in groups] + [cur(1, g) for g in groups] + [prev(1, g) for g in groups]
                + [cur(2, g) for g in groups] + [prev(2, g) for g in groups]
                + [pl.BlockSpec((2, span, LANES), lambda b, n, h: (0, b * nspan + n, 0)),
                   pl.BlockSpec((2, span, LANES), lambda b, n, h: (0, jnp.maximum(b * nspan + n - 1, 0), 0))])
    return pl.pallas_call(
        functools.partial(_dilated_attn_kernel, dilations=dilations, n_backs=n_backs),
        out_shape=jax.ShapeDtypeStruct((t, gw), bf16),
        grid=(batch, nspan, heads),
        in_specs=in_specs,
        out_specs=pl.BlockSpec((span, A_HEAD_DIM), lambda b, n, h: (b * nspan + n, h)),
        scratch_shapes=[pltpu.VMEM((span, A_HEAD_DIM), f32)] * (2 * ng),
        compiler_params=_params(("parallel", "parallel", "arbitrary")),
        name="dilated_attn",
    )(*([p_in] * (5 * ng)), tabs, tabs)


def _xattn_kernel(q_ref, k_ref, v_ref, o_ref, *, heads):
    hd = q_ref.shape[1] // heads
    scale = hd ** -0.5
    for h in range(heads):
        sl = slice(h * hd, (h + 1) * hd)
        s = _dot_nt(q_ref[:, sl], k_ref[:, sl]) * scale
        m = jnp.max(s, axis=-1, keepdims=True)
        p = jnp.exp(s - m)
        l = jnp.sum(p, axis=-1, keepdims=True)
        o_ref[:, sl] = _dot(p / l, v_ref[:, sl]).astype(o_ref.dtype)


def cross_attn(q_arr, q_col_block, mem_kv, batch, seq_len, tq=512):
    xw = mem_kv.shape[1] // 2
    mem_len = mem_kv.shape[0] // batch
    tq = _pick_tile(seq_len, tq)
    nq = seq_len // tq
    return pl.pallas_call(
        functools.partial(_xattn_kernel, heads=X_HEADS),
        out_shape=jax.ShapeDtypeStruct((batch * seq_len, xw), bf16),
        grid=(batch, nq),
        in_specs=[pl.BlockSpec((tq, xw), lambda b, i: (b * nq + i, q_col_block)),
                  pl.BlockSpec((mem_len, xw), lambda b, i: (b, 0)),
                  pl.BlockSpec((mem_len, xw), lambda b, i: (b, 1))],
        out_specs=pl.BlockSpec((tq, xw), lambda b, i: (b * nq + i, 0)),
        compiler_params=_params(("parallel", "parallel")),
        name="cross_attn",
    )(q_arr, mem_kv, mem_kv)


def _shift(x, halo, mu, seq_start):
    row = lax.broadcasted_iota(jnp.int32, x.shape, 0)
    last = jnp.where(seq_start, 0.0, halo[SUBLANES - 1:SUBLANES, :])
    prev = jnp.where(row == 0, last, pltpu.roll(x, 1, 0))
    return x + (prev - x) * mu


def _rwkv_prep_kernel(pr, pk, pv, hr, hk, hv, lo, hlo, vf,
                      mu_r, mu_k, mu_v, mu_l, w0, a0, v0, k_k, k_a, wup, aup, vup, gup,
                      r_o, lw_o, k_o, v_o, kk_o, b_o, g_o, *, tiles_per_seq):
    seq_start = pl.program_id(1) % tiles_per_seq == 0
    r = _shift(pr[...], hr[...], mu_r[...], seq_start)
    k = _shift(pk[...], hk[...], mu_k[...], seq_start)
    v = _shift(pv[...], hv[...], mu_v[...], seq_start)
    low = _shift(lo[...], hlo[...], mu_l[...], seq_start)

    xw = w0[...] + _dot(jnp.tanh(low), wup[...])
    lw_o[...] = -math.exp(-0.5) * _sigmoid(xw)
    ag = _sigmoid(a0[...] + _dot(low, aup[...]))
    v_gate = _sigmoid(v0[...] + _dot(low, vup[...]))
    g_o[...] = _dot(_sigmoid(low), gup[...])
    v_o[...] = v + (vf[...] - v) * v_gate
    r_o[...] = r

    kk = k * k_k[...]
    ones = _head_block_ones(kk.shape[1], R_HEAD_DIM)
    norm = jnp.sqrt(_split_dot(kk * kk, ones))
    kk = kk / jnp.maximum(norm, L2_EPS)
    kk_o[...] = kk
    b_o[...] = kk * ag
    k_o[...] = k * (1.0 + (ag - 1.0) * k_a[...])


def rwkv_prep(p_rkv, p_low, p_a, mix_w, seq_len, mu, w0, a0, v0, k_k, k_a,
              wup_pad, aup_pad, vup_pad, gup_pad, *, tm=256, tc=512):
    t = p_rkv.shape[0]
    lw_dim = p_low.shape[1]
    tm, tc = _pick_tile(seq_len, tm), _pick_tile(mix_w, tc)
    assert seq_len % tm == 0 and mix_w % tc == 0 and tc % LANES == 0
    nc = mix_w // tc
    hb = tm // SUBLANES

    def tile(cb):
        return pl.BlockSpec((tm, tc), lambda c, i: (i, cb * nc + c))

    def halo(cb):
        return pl.BlockSpec((SUBLANES, tc), lambda c, i: (jnp.maximum(i * hb - 1, 0), cb * nc + c))

    def row(cb=0):
        return pl.BlockSpec((1, tc), lambda c, i: (0, cb * nc + c))

    low_w = pl.BlockSpec((lw_dim, tc), lambda c, i: (0, c))
    out_spec = pl.BlockSpec((tm, tc), lambda c, i: (i, c))
    out_sds = jax.ShapeDtypeStruct((t, mix_w), f32)
    mu_full = mu.reshape(1, -1)
    mu_low = mu[3 * mix_w:].reshape(1, lw_dim)
    in_specs = [tile(0), tile(1), tile(2), halo(0), halo(1), halo(2),
                pl.BlockSpec((tm, lw_dim), lambda c, i: (i, 0)),
                pl.BlockSpec((SUBLANES, lw_dim), lambda c, i: (jnp.maximum(i * hb - 1, 0), 0)),
                tile(2),
                row(0), row(1), row(2),
                pl.BlockSpec((1, lw_dim), lambda c, i: (0, 0)),
                row(), row(), row(), row(), row(),
                low_w, low_w, low_w, low_w]
    return pl.pallas_call(
        functools.partial(_rwkv_prep_kernel, tiles_per_seq=seq_len // tm),
        out_shape=(out_sds,) * 7,
        grid=(nc, t // tm),
        in_specs=in_specs,
        out_specs=(out_spec,) * 7,
        compiler_params=_params(("parallel", "parallel")),
        name="rwkv_prep",
    )(p_rkv, p_rkv, p_rkv, p_rkv, p_rkv, p_rkv, p_low, p_low, p_a,
      mu_full, mu_full, mu_full, mu_low,
      w0.reshape(1, -1), a0.reshape(1, -1), v0.reshape(1, -1), k_k.reshape(1, -1),
      k_a.reshape(1, -1), wup_pad, aup_pad, vup_pad, gup_pad)


def _rwkv_scan_kernel(r_ref, lw_ref, k_ref, v_ref, kk_ref, b_ref, g_ref, lng_ref, lnb_ref, rk_ref,
                      o_ref, state_ref):
    @pl.when(pl.program_id(2) == 0)
    def _():
        state_ref[...] = jnp.zeros_like(state_ref)

    c = SCAN_CHUNK
    n = 2 * c
    pairs = range(r_ref.shape[1] // LANES)
    cols = [slice(p * LANES, (p + 1) * LANES) for p in pairs]

    tri_r = lax.broadcasted_iota(jnp.int32, (c, c), 0)
    tri_c = lax.broadcasted_iota(jnp.int32, (c, c), 1)
    tri = jnp.where(tri_r >= tri_c, 1.0, 0.0)
    lw_all = lw_ref[...]
    lcum_all = jnp.dot(tri, lw_all, preferred_element_type=f32, precision=lax.Precision.HIGHEST)

    head0 = lax.broadcasted_iota(jnp.int32, (c, LANES), 1) < R_HEAD_DIM
    row = lax.broadcasted_iota(jnp.int32, (n, n), 0)
    col = lax.broadcasted_iota(jnp.int32, (n, n), 1)
    rin, cin = row & (c - 1), col & (c - 1)
    strict, incl = rin > cin, rin >= cin
    eye = jnp.where(row == col, 1.0, 0.0)

    def merge_mask(level):
        rb, cb = row >> level, col >> level
        return ((rb >> 1) == (cb >> 1)) & ((rb & 1) == 1) & ((cb & 1) == 0)

    def stack(x):
        return jnp.concatenate([jnp.where(head0, x, 0.0), jnp.where(head0, 0.0, x)], axis=0)

    ops = []
    for sl in cols:
        lcum, lw = lcum_all[:, sl], lw_all[:, sl]
        r, k, v, kk, b = r_ref[:, sl], k_ref[:, sl], v_ref[:, sl], kk_ref[:, sl], b_ref[:, sl]
        l_end = lcum[c - 1:c, :]
        e_in = jnp.exp(lcum)
        e_out = jnp.exp(-lcum)
        e_end = jnp.exp(l_end - lcum)
        ops.append(dict(
            a_t=stack(-kk * jnp.exp(lcum - lw)), r_t=stack(r * e_in),
            b_t=stack(b * e_out), k_t=stack(k * e_out),
            bk_h=jnp.concatenate([stack(b * e_end), stack(k * e_end)], axis=0),
            v_s=stack(v), decay=jnp.exp(l_end)))

    for o in ops:
        s1 = _dot_nt(jnp.concatenate([o["a_t"], o["r_t"]], axis=0),
                     jnp.concatenate([o["b_t"], o["k_t"]], axis=0))
        o["ab"] = jnp.where(strict, s1[:n, :n], 0.0)
        o["ak"] = jnp.where(strict, s1[:n, n:], 0.0)
        o["rbk"] = jnp.concatenate([jnp.where(incl, s1[n:, :n], 0.0),
                                    jnp.where(incl, s1[n:, n:], 0.0)], axis=1)

    mask = merge_mask(0)
    ts = [eye + jnp.where(mask, o["ab"], 0.0) for o in ops]
    for level in range(1, int(math.log2(c))):
        mask = merge_mask(level)
        zs = [_dot(jnp.where(mask, o["ab"], 0.0), t) for o, t in zip(ops, ts)]
        ts = [t + _dot(t, z) for t, z in zip(ts, zs)]

    states = [state_ref[p] for p in pairs]
    rhs = [_dot_nt(o["a_t"], s) + _dot(o["ak"], o["v_s"]) for o, s in zip(ops, states)]
    us = [_dot(t, x) for t, x in zip(ts, rhs)]
    uvs = [jnp.concatenate([u, o["v_s"]], axis=0) for u, o in zip(us, ops)]
    y_stacked = [_dot_nt(o["r_t"], s) + _dot(o["rbk"], uv) for o, s, uv in zip(ops, states, uvs)]
    for p, o, s, uv in zip(pairs, ops, states, uvs):
        state_ref[p] = s * o["decay"] + _dot_tn(uv, o["bk_h"])

    ones = _head_block_ones(LANES, R_HEAD_DIM)
    inv_n = 1.0 / R_HEAD_DIM
    ys = [y[:c, :] + y[c:, :] for y in y_stacked]
    mus = [_split_dot(y, ones) * inv_n for y in ys]
    ycs = [y - mu for y, mu in zip(ys, mus)]
    vars_ = [_split_dot(yc * yc, ones) * inv_n for yc in ycs]
    bonus = [_split_dot(r_ref[:, sl] * k_ref[:, sl] * rk_ref[:, sl], ones) for sl in cols]
    for sl, yc, var, bo in zip(cols, ycs, vars_, bonus):
        y = yc * lax.rsqrt(var + GN_EPS) * lng_ref[:, sl] + lnb_ref[:, sl]
        o_ref[:, sl] = ((y + bo * v_ref[:, sl]) * g_ref[:, sl]).astype(o_ref.dtype)


def rwkv_scan(r, lw, k, v, kk, b, g, ln_g, ln_b, r_k, batch, seq_len, *, wb=1024):
    t, mix_w = r.shape
    wb = _pick_tile(mix_w, wb)
    assert mix_w % wb == 0 and wb % LANES == 0 and seq_len % SCAN_CHUNK == 0
    nchunk = seq_len // SCAN_CHUNK
    tile = pl.BlockSpec((SCAN_CHUNK, wb), lambda bi, w, n: (bi * nchunk + n, w))
    row = pl.BlockSpec((1, wb), lambda bi, w, n: (0, w))
    return pl.pallas_call(
        _rwkv_scan_kernel,
        out_shape=jax.ShapeDtypeStruct((t, mix_w), bf16),
        grid=(batch, mix_w // wb, nchunk),
        in_specs=[tile] * 7 + [row] * 3,
        out_specs=tile,
        scratch_shapes=[pltpu.VMEM((wb // LANES, LANES, LANES), f32)],
        compiler_params=_params(("parallel", "parallel", "arbitrary")),
        name="rwkv_scan",
    )(r, lw, k, v, kk, b, g, ln_g.reshape(1, -1), ln_b.reshape(1, -1), r_k.reshape(1, -1))


def _pad_rows(w, start, total):
    return jnp.zeros((total, w.shape[1]), w.dtype).at[start:start + w.shape[0]].set(w)


def kernel(x, mem, positions, mem_norm_g, mem_w_kv, a_norm_g, a_w_in, a_w_out, b_norm_g, b_w_in, b_shift_mu, b_w0, b_w_up, b_a0, b_a_up, b_v0, b_v_up, b_g_up, b_k_k, b_k_a, b_r_k, b_ln_g, b_ln_b, b_w_out, ffn_norm_g, ffn_w_up, ffn_conv_w, ffn_conv_b, ffn_w_down, final_norm_g):
    batch, seq_len, d_model = x.shape
    t = batch * seq_len
    depth = ffn_w_up.shape[0]
    xw = d_model // 4
    mix_w = d_model - xw
    x = x.reshape(t, d_model)

    mem_h = rmsnorm(mem.reshape(-1, d_model), mem_norm_g, bf16)
    mem_kv = matmul(mem_h, mem_w_kv[None], 0)
    tabs = rope_tables(positions)

    p_a = None
    for i in range(depth):
        j = i // 2
        if i % 2 == 0:
            h = rmsnorm(x, a_norm_g[j], bf16)
            p = matmul(h, a_w_in, j)
            if p_a is None:
                p_a = p
            mix = dilated_attn(p, tabs, batch, seq_len)
            xo = cross_attn(p, 3 * mix_w // xw, mem_kv, batch, seq_len)
            w_out = a_w_out
        else:
            h = rmsnorm(x, b_norm_g[j], bf16)
            low_w = b_w_in.shape[2] - 3 * mix_w - xw
            p_rkv = matmul(h, b_w_in, j, n_cols=3 * mix_w)
            w_tail = b_w_in[j][:, 3 * mix_w:]
            p_low = matmul(h, w_tail[None, :, :low_w], 0)
            cq = matmul(h, w_tail[None, :, low_w:], 0, out_dtype=bf16)
            dd, da, dv = b_w_up.shape[1], b_a_up.shape[1], b_v_up.shape[1]
            prep = rwkv_prep(
                p_rkv, p_low, p_a, mix_w, seq_len, b_shift_mu[j], b_w0[j], b_a0[j], b_v0[j],
                b_k_k[j], b_k_a[j],
                _pad_rows(b_w_up[j], 0, low_w), _pad_rows(b_a_up[j], dd, low_w),
                _pad_rows(b_v_up[j], dd + da, low_w), _pad_rows(b_g_up[j], dd + da + dv, low_w))
            mix = rwkv_scan(*prep, b_ln_g[j], b_ln_b[j], b_r_k[j], batch, seq_len)
            xo = cross_attn(cq, 0, mem_kv, batch, seq_len)
            w_out = b_w_out
        x = out_proj(mix, xo, w_out, j, x)
        h = rmsnorm(x, ffn_norm_g[i], bf16)
        act = ffn_up(h, ffn_w_up, ffn_conv_w, ffn_conv_b, i, seq_len)
        x = matmul(act, ffn_w_down, i, res=x, tn=1024, tk=2048)
    return rmsnorm(x, final_norm_g, f32).reshape(batch, seq_len, d_model)
```

```python
import functools
import math

import jax
import jax.numpy as jnp
from jax import lax
from jax.experimental import pallas as pl
from jax.experimental.pallas import tpu as pltpu

f32 = jnp.float32
bf16 = jnp.bfloat16

X_HEADS = 4
A_HEAD_DIM = 128
A_GROUPS = ((128, 1), (512, 4), (2048, 16))
ATT_BLOCK = 128
ATT_UNITS_PER_BATCH = 4
ROT_DIM = A_HEAD_DIM // 4
ROPE_THETA = 500000.0
R_HEAD_DIM = 64
CONV_WIDTH = 3
NORM_EPS = 1e-6
GN_EPS = 64e-5
L2_EPS = 1e-12
NEG_INF = -1e30

V7X_VMEM_BYTES = 64 * 2**20
VMEM_LIMIT_BYTES = V7X_VMEM_BYTES - 8 * 2**20
SUBLANES = 8
LANES = 128

SCAN_CHUNK = LANES // 2
HEADS_PER_LANE_GROUP = LANES // R_HEAD_DIM


def _pick_tile(n, preferred):
    if n % preferred == 0:
        return preferred
    g = math.gcd(n, preferred)
    return g if g % LANES == 0 else n


def _params(semantics):
    return pltpu.CompilerParams(dimension_semantics=semantics,
                                vmem_limit_bytes=VMEM_LIMIT_BYTES)


def _dot(a, b):
    return jnp.dot(a.astype(bf16), b.astype(bf16), preferred_element_type=f32)


def _dot_nt(a, b):
    return lax.dot_general(a.astype(bf16), b.astype(bf16), (((1,), (1,)), ((), ())),
                           preferred_element_type=f32)


def _dot_tn(a, b):
    return lax.dot_general(a.astype(bf16), b.astype(bf16), (((0,), (0,)), ((), ())),
                           preferred_element_type=f32)


def _sigmoid(x):
    return 1.0 / (1.0 + jnp.exp(-x))


def _split_dot(x, ones_mat):
    hi = x.astype(bf16)
    lo = (x - hi.astype(f32)).astype(bf16)
    return (jnp.dot(hi, ones_mat, preferred_element_type=f32)
            + jnp.dot(lo, ones_mat, preferred_element_type=f32))


def _head_block_ones(width, head_dim):
    shift = int(math.log2(head_dim))
    r = lax.broadcasted_iota(jnp.int32, (width, width), 0) >> shift
    c = lax.broadcasted_iota(jnp.int32, (width, width), 1) >> shift
    return jnp.where(r == c, 1.0, 0.0).astype(bf16)


def _rmsnorm_kernel(x_ref, g_ref, o_ref):
    x = x_ref[...].astype(f32)
    ms = jnp.mean(x * x, axis=-1, keepdims=True)
    o_ref[...] = (x * lax.rsqrt(ms + NORM_EPS) * g_ref[...]).astype(o_ref.dtype)


def rmsnorm(x, g, out_dtype, tm=256):
    m, d = x.shape
    tm = min(tm, m)
    return pl.pallas_call(
        _rmsnorm_kernel,
        out_shape=jax.ShapeDtypeStruct((m, d), out_dtype),
        grid=(m // tm,),
        in_specs=[pl.BlockSpec((tm, d), lambda i: (i, 0)),
                  pl.BlockSpec((1, d), lambda i: (0, 0))],
        out_specs=pl.BlockSpec((tm, d), lambda i: (i, 0)),
        compiler_params=_params(("parallel",)),
        name="rmsnorm",
    )(x, g.reshape(1, d))


def _matmul_kernel(*refs, nk, has_res):
    if has_res:
        a_ref, w_ref, res_ref, o_ref = refs[:4]
        scratch = refs[4:]
    else:
        a_ref, w_ref, o_ref = refs[:3]
        res_ref = None
        scratch = refs[3:]
    if nk == 1:
        out = _dot(a_ref[...], w_ref[...])
        if has_res:
            out = out + res_ref[...]
        o_ref[...] = out.astype(o_ref.dtype)
        return
    acc_ref, = scratch
    k = pl.program_id(2)

    @pl.when(k == 0)
    def _():
        acc_ref[...] = res_ref[...] if has_res else jnp.zeros_like(acc_ref)

    acc_ref[...] += _dot(a_ref[...], w_ref[...])

    @pl.when(k == nk - 1)
    def _():
        o_ref[...] = acc_ref[...].astype(o_ref.dtype)


def matmul(a, w, layer, *, res=None, out_dtype=f32, tm=1024, tn=512, tk=None):
    m, k_dim = a.shape
    n = w.shape[2]
    tm, tn = _pick_tile(m, tm), _pick_tile(n, tn)
    tk = k_dim if tk is None else _pick_tile(k_dim, tk)
    assert m % tm == 0 and n % tn == 0 and k_dim % tk == 0, (a.shape, w.shape, tm, tn, tk)
    nk = k_dim // tk
    in_specs = [pl.BlockSpec((tm, tk), lambda i, j, k: (i, k)),
                pl.BlockSpec((None, tk, tn), lambda i, j, k: (layer, k, j))]
    args = [a, w]
    if res is not None:
        in_specs.append(pl.BlockSpec((tm, tn), lambda i, j, k: (i, j)))
        args.append(res)
    return pl.pallas_call(
        functools.partial(_matmul_kernel, nk=nk, has_res=res is not None),
        out_shape=jax.ShapeDtypeStruct((m, n), out_dtype),
        grid=(m // tm, n // tn, nk),
        in_specs=in_specs,
        out_specs=pl.BlockSpec((tm, tn), lambda i, j, k: (i, j)),
        scratch_shapes=[pltpu.VMEM((tm, tn), f32)] if nk > 1 else [],
        compiler_params=_params(("parallel", "parallel", "arbitrary")),
        name="matmul",
    )(*args)


def _out_proj_kernel(mix_ref, xo_ref, w_ref, res_ref, o_ref):
    k_mix = mix_ref.shape[1]
    o_ref[...] = (res_ref[...] + _dot(mix_ref[...], w_ref[:k_mix, :])
                  + _dot(xo_ref[...], w_ref[k_mix:, :]))


def out_proj(mix, xo, w, layer, res, *, tm=1024, tn=512):
    m, k_mix = mix.shape
    k_xo = xo.shape[1]
    n = w.shape[2]
    tm, tn = _pick_tile(m, tm), _pick_tile(n, tn)
    assert w.shape[1] == k_mix + k_xo
    return pl.pallas_call(
        _out_proj_kernel,
        out_shape=jax.ShapeDtypeStruct((m, n), f32),
        grid=(m // tm, n // tn),
        in_specs=[pl.BlockSpec((tm, k_mix), lambda i, j: (i, 0)),
                  pl.BlockSpec((tm, k_xo), lambda i, j: (i, 0)),
                  pl.BlockSpec((None, k_mix + k_xo, tn), lambda i, j: (layer, 0, j)),
                  pl.BlockSpec((tm, tn), lambda i, j: (i, j))],
        out_specs=pl.BlockSpec((tm, tn), lambda i, j: (i, j)),
        compiler_params=_params(("parallel", "parallel")),
        name="out_proj",
    )(mix, xo, w, res)


def _ffn_up_kernel(a_ref, wg_ref, wu_ref, cw_ref, cb_ref, o_ref, wg_bf, wu_bf, tail_ref,
                   *, tiles_per_seq):
    i = pl.program_id(1)

    @pl.when(i == 0)
    def _():
        wg_bf[...] = wg_ref[...].astype(bf16)
        wu_bf[...] = wu_ref[...].astype(bf16)

    a = a_ref[...]
    gate = jnp.dot(a, wg_bf[...], preferred_element_type=f32)
    up = jnp.dot(a, wu_bf[...], preferred_element_type=f32)
    tm = gate.shape[0]

    tail = jnp.where(i % tiles_per_seq == 0, 0.0, tail_ref[...])
    tail_ref[...] = gate[tm - SUBLANES:, :]

    row = lax.broadcasted_iota(jnp.int32, gate.shape, 0)
    last, last2 = tail[SUBLANES - 1:SUBLANES, :], tail[SUBLANES - 2:SUBLANES - 1, :]
    g1 = jnp.where(row == 0, last, pltpu.roll(gate, 1, 0))
    g2 = jnp.where(row == 0, last2, jnp.where(row == 1, last, pltpu.roll(gate, 2, 0)))
    cw = cw_ref[...]
    conv = cb_ref[...] + cw[0:1, :] * g2 + cw[1:2, :] * g1 + cw[2:3, :] * gate
    o_ref[...] = (conv * _sigmoid(conv) * up).astype(o_ref.dtype)


def ffn_up(h, w_up, conv_w, conv_b, layer, seq_len, *, tm=512, tn=512):
    m, d = h.shape
    d_ff = w_up.shape[2] // 2
    tm, tn = _pick_tile(seq_len, tm), _pick_tile(d_ff, tn)
    assert m % tm == 0 and seq_len % tm == 0 and d_ff % tn == 0
    nj = d_ff // tn
    depth = conv_w.shape[0]
    return pl.pallas_call(
        functools.partial(_ffn_up_kernel, tiles_per_seq=seq_len // tm),
        out_shape=jax.ShapeDtypeStruct((m, d_ff), bf16),
        grid=(nj, m // tm),
        in_specs=[pl.BlockSpec((tm, d), lambda j, i: (i, 0)),
                  pl.BlockSpec((None, d, tn), lambda j, i: (layer, 0, j)),
                  pl.BlockSpec((None, d, tn), lambda j, i: (layer, 0, j + nj)),
                  pl.BlockSpec((None, CONV_WIDTH, tn), lambda j, i: (layer, 0, j)),
                  pl.BlockSpec((None, 1, tn), lambda j, i: (layer, 0, j))],
        out_specs=pl.BlockSpec((tm, tn), lambda j, i: (i, j)),
        scratch_shapes=[pltpu.VMEM((d, tn), bf16), pltpu.VMEM((d, tn), bf16),
                        pltpu.VMEM((SUBLANES, tn), f32)],
        compiler_params=_params(("arbitrary", "arbitrary")),
        name="ffn_up",
    )(h, w_up, w_up, conv_w, conv_b.reshape(depth, 1, d_ff))


def _rope_table_kernel(pos_ref, freq_ref, o_ref):
    ang = pos_ref[...].astype(f32) * freq_ref[...]
    lane = lax.broadcasted_iota(jnp.int32, ang.shape, 1)
    o_ref[0] = jnp.where(lane < ROT_DIM, jnp.cos(ang), 1.0)
    o_ref[1] = jnp.where(lane < ROT_DIM, jnp.sin(ang), 0.0)


def rope_tables(positions, tm=512):
    t = positions.size
    half = ROT_DIM // 2
    inv_freq = ROPE_THETA ** (-jnp.arange(half, dtype=f32) / half)
    freq = jnp.zeros((1, A_HEAD_DIM), f32).at[0, :ROT_DIM].set(jnp.tile(inv_freq, 2))
    tm = _pick_tile(t, tm)
    return pl.pallas_call(
        _rope_table_kernel,
        out_shape=jax.ShapeDtypeStruct((2, t, LANES), f32),
        grid=(t // tm,),
        in_specs=[pl.BlockSpec((tm, 1), lambda i: (i, 0)),
                  pl.BlockSpec((1, A_HEAD_DIM), lambda i: (0, 0))],
        out_specs=pl.BlockSpec((2, tm, LANES), lambda i: (0, i, 0)),
        compiler_params=_params(("parallel",)),
        name="rope_tables",
    )(positions.reshape(t, 1), freq)


def _rope(x, cos, sin):
    half = ROT_DIM // 2
    lane = lax.broadcasted_iota(jnp.int32, x.shape, 1)
    rot = jnp.where(lane < half, -pltpu.roll(x, A_HEAD_DIM - half, 1), pltpu.roll(x, half, 1))
    return x * cos + rot * sin


def _dilated_attn_kernel(*refs, dilations, n_backs):
    ng = len(dilations)
    q_refs, kc_refs, kp_refs = refs[0:ng], refs[ng:2 * ng], refs[2 * ng:3 * ng]
    vc_refs, vp_refs = refs[3 * ng:4 * ng], refs[4 * ng:5 * ng]
    tc_ref, tp_ref, o_ref = refs[5 * ng:5 * ng + 3]
    o_scr, lse_scr = refs[5 * ng + 3:6 * ng + 3], refs[6 * ng + 3:7 * ng + 3]

    blk = ATT_BLOCK
    span = q_refs[0].shape[0]
    first_span = pl.program_id(1) == 0
    qi = lax.broadcasted_iota(jnp.int32, (blk, 2 * blk), 0)
    col = lax.broadcasted_iota(jnp.int32, (blk, 2 * blk), 1)
    dist = qi + blk - col
    first_col = jnp.where(first_span, blk, 0)
    scale = A_HEAD_DIM ** -0.5

    def rows(start, size, d):
        return pl.ds(start, size, stride=d) if d > 1 else pl.ds(start, size)

    def load_unit(g, d, r, j):
        span_g = d * blk
        cur = rows(j * span_g + r, blk, d)
        if j == 0:
            prev = rows(r, blk, d)
            prev_tab = rows(span - span_g + r, blk, d)
            cos_c, sin_c = tc_ref[0, cur, :], tc_ref[1, cur, :]
            k2 = jnp.concatenate(
                [_rope(kp_refs[g][prev, :], tp_ref[0, prev_tab, :], tp_ref[1, prev_tab, :]),
                 _rope(kc_refs[g][cur, :], cos_c, sin_c)], axis=0)
            v2 = jnp.concatenate([vp_refs[g][prev, :], vc_refs[g][cur, :]], axis=0)
        else:
            both = rows((j - 1) * span_g + r, 2 * blk, d)
            cos2, sin2 = tc_ref[0, both, :], tc_ref[1, both, :]
            k2 = _rope(kc_refs[g][both, :], cos2, sin2)
            v2 = vc_refs[g][both, :]
            cos_c, sin_c = cos2[blk:, :], sin2[blk:, :]
        q = _rope(q_refs[g][cur, :], cos_c, sin_c) * scale
        return q.astype(bf16), k2.astype(bf16), v2.astype(bf16)

    for g, (d, n_back) in enumerate(zip(dilations, n_backs)):
        in_window = (dist >= 0) & (dist <= n_back)
        in_window_first = in_window & (col >= first_col)
        units = [(r, j) for r in range(d) for j in range(span // (d * blk))]
        for b0 in range(0, len(units), ATT_UNITS_PER_BATCH):
            batch_units = units[b0:b0 + ATT_UNITS_PER_BATCH]
            loaded = [load_unit(g, d, r, j) for r, j in batch_units]
            scores = [jnp.where(in_window_first if j == 0 else in_window, _dot_nt(q, k2), NEG_INF)
                      for (q, k2, _), (_, j) in zip(loaded, batch_units)]
            maxes = [jnp.max(s, axis=-1, keepdims=True) for s in scores]
            probs = [jnp.exp(s - m) for s, m in zip(scores, maxes)]
            sums = [jnp.sum(p, axis=-1, keepdims=True) for p in probs]
            outs = [_dot(p, v2) * (1.0 / l) for p, (_, _, v2), l in zip(probs, loaded, sums)]
            for (r, j), o, m, l in zip(batch_units, outs, maxes, sums):
                cur = rows(j * d * blk + r, blk, d)
                o_scr[g][cur, :] = o
                lse_scr[g][cur, :] = jnp.broadcast_to(m + jnp.log(l), (blk, A_HEAD_DIM))

    lses = [ref[...] for ref in lse_scr]
    m = functools.reduce(jnp.maximum, lses)
    wts = [jnp.exp(x - m) for x in lses]
    num = sum(w * ref[...] for w, ref in zip(wts, o_scr))
    o_ref[...] = (num / sum(wts)).astype(o_ref.dtype)


def dilated_attn(p_in, tabs, batch, seq_len):
    t, ncol = p_in.shape
    ng = len(A_GROUPS)
    gw = ncol // (3 * ng + 1)
    heads = gw // A_HEAD_DIM
    dilations = tuple(d for _, d in A_GROUPS)
    n_backs = tuple(w // d for w, d in A_GROUPS)
    span = max(dilations) * ATT_BLOCK
    assert all(nb <= ATT_BLOCK for nb in n_backs) and seq_len % span == 0
    assert all(span % (d * ATT_BLOCK) == 0 for d in dilations)
    nspan = seq_len // span

    def cur(part, g):
        return pl.BlockSpec((span, A_HEAD_DIM),
                            lambda b, n, h: (b * nspan + n, (part * ng + g) * heads + h))

    def prev(part, g):
        per_span = span // (dilations[g] * ATT_BLOCK)
        return pl.BlockSpec(
            (dilations[g] * ATT_BLOCK, A_HEAD_DIM),
            lambda b, n, h: (jnp.maximum((b * nspan + n) * per_span - 1, 0), (part * ng + g) * heads + h))

    groups = range(ng)
    in_specs = ([cur(0, g) for g in groups] + [cur(1, g) for g in groups] + [prev(1, g) for g in groups]
                + [cur(2, g) for g in groups] + [prev(2, g) for g in groups]
                + [pl.BlockSpec((2, span, LANES), lambda b, n, h: (0, b * nspan + n, 0)),
                   pl.BlockSpec((2, span, LANES), lambda b, n, h: (0, jnp.maximum(b * nspan + n - 1, 0), 0))])
    return pl.pallas_call(
        functools.partial(_dilated_attn_kernel, dilations=dilations, n_backs=n_backs),
        out_shape=jax.ShapeDtypeStruct((t, gw), bf16),
        grid=(batch, nspan, heads),
        in_specs=in_specs,
        out_specs=pl.BlockSpec((span, A_HEAD_DIM), lambda b, n, h: (b * nspan + n, h)),
        scratch_shapes=[pltpu.VMEM((span, A_HEAD_DIM), f32)] * (2 * ng),
        compiler_params=_params(("parallel", "parallel", "arbitrary")),
        name="dilated_attn",
    )(*([p_in] * (5 * ng)), tabs, tabs)


def _xattn_kernel(q_ref, k_ref, v_ref, o_ref, *, heads):
    hd = q_ref.shape[1] // heads
    scale = hd ** -0.5
    for h in range(heads):
        sl = slice(h * hd, (h + 1) * hd)
        s = _dot_nt(q_ref[:, sl], k_ref[:, sl]) * scale
        m = jnp.max(s, axis=-1, keepdims=True)
        p = jnp.exp(s - m)
        l = jnp.sum(p, axis=-1, keepdims=True)
        o_ref[:, sl] = _dot(p / l, v_ref[:, sl]).astype(o_ref.dtype)


def cross_attn(q_arr, q_col_block, mem_kv, batch, seq_len, tq=512):
    xw = mem_kv.shape[1] // 2
    mem_len = mem_kv.shape[0] // batch
    tq = _pick_tile(seq_len, tq)
    nq = seq_len // tq
    return pl.pallas_call(
        functools.partial(_xattn_kernel, heads=X_HEADS),
        out_shape=jax.ShapeDtypeStruct((batch * seq_len, xw), bf16),
        grid=(batch, nq),
        in_specs=[pl.BlockSpec((tq, xw), lambda b, i: (b * nq + i, q_col_block)),
                  pl.BlockSpec((mem_len, xw), lambda b, i: (b, 0)),
                  pl.BlockSpec((mem_len, xw), lambda b, i: (b, 1))],
        out_specs=pl.BlockSpec((tq, xw), lambda b, i: (b * nq + i, 0)),
        compiler_params=_params(("parallel", "parallel")),
        name="cross_attn",
    )(q_arr, mem_kv, mem_kv)


def _shift(x, halo, mu, seq_start):
    row = lax.broadcasted_iota(jnp.int32, x.shape, 0)
    last = jnp.where(seq_start, 0.0, halo[SUBLANES - 1:SUBLANES, :])
    prev = jnp.where(row == 0, last, pltpu.roll(x, 1, 0))
    return x + (prev - x) * mu


def _rwkv_prep_kernel(pr, pk, pv, hr, hk, hv, lo, hlo, vf,
                      mu_r, mu_k, mu_v, mu_l, w0, a0, v0, k_k, k_a, wup, aup, vup, gup,
                      r_o, lw_o, k_o, v_o, kk_o, b_o, g_o, low_s, low_tanh_s, low_sig_s,
                      *, tiles_per_seq):
    seq_start = pl.program_id(0) % tiles_per_seq == 0

    @pl.when(pl.program_id(1) == 0)
    def _():
        low = _shift(lo[...], hlo[...], mu_l[...], seq_start)
        low_s[...] = low.astype(bf16)
        low_tanh_s[...] = jnp.tanh(low).astype(bf16)
        low_sig_s[...] = _sigmoid(low).astype(bf16)

    r = _shift(pr[...], hr[...], mu_r[...], seq_start)
    k = _shift(pk[...], hk[...], mu_k[...], seq_start)
    v = _shift(pv[...], hv[...], mu_v[...], seq_start)

    low = low_s[...]
    xw = w0[...] + _dot(low_tanh_s[...], wup[...])
    lw_o[...] = -math.exp(-0.5) * _sigmoid(xw)
    ag = _sigmoid(a0[...] + _dot(low, aup[...]))
    v_gate = _sigmoid(v0[...] + _dot(low, vup[...]))
    g_o[...] = _dot(low_sig_s[...], gup[...])
    v_o[...] = v + (vf[...] - v) * v_gate
    r_o[...] = r

    kk = k * k_k[...]
    ones = _head_block_ones(kk.shape[1], R_HEAD_DIM)
    norm = jnp.sqrt(_split_dot(kk * kk, ones))
    kk = kk / jnp.maximum(norm, L2_EPS)
    kk_o[...] = kk
    b_o[...] = kk * ag
    k_o[...] = k * (1.0 + (ag - 1.0) * k_a[...])


def rwkv_prep(p_rkv, p_low, p_a, mix_w, seq_len, mu, w0, a0, v0, k_k, k_a,
              wup_pad, aup_pad, vup_pad, gup_pad, *, tm=512, tc=512):
    t = p_rkv.shape[0]
    lw_dim = p_low.shape[1]
    tm, tc = _pick_tile(seq_len, tm), _pick_tile(mix_w, tc)
    assert seq_len % tm == 0 and mix_w % tc == 0 and tc % LANES == 0
    nc = mix_w // tc
    hb = tm // SUBLANES

    def tile(cb):
        return pl.BlockSpec((tm, tc), lambda i, c: (i, cb * nc + c))

    def halo(cb):
        return pl.BlockSpec((SUBLANES, tc), lambda i, c: (jnp.maximum(i * hb - 1, 0), cb * nc + c))

    def row(cb=0):
        return pl.BlockSpec((1, tc), lambda i, c: (0, cb * nc + c))

    low_w = pl.BlockSpec((lw_dim, tc), lambda i, c: (0, c))
    out_spec = pl.BlockSpec((tm, tc), lambda i, c: (i, c))
    out_sds = jax.ShapeDtypeStruct((t, mix_w), f32)
    mu_full = mu.reshape(1, -1)
    mu_low = mu[3 * mix_w:].reshape(1, lw_dim)
    in_specs = [tile(0), tile(1), tile(2), halo(0), halo(1), halo(2),
                pl.BlockSpec((tm, lw_dim), lambda i, c: (i, 0)),
                pl.BlockSpec((SUBLANES, lw_dim), lambda i, c: (jnp.maximum(i * hb - 1, 0), 0)),
                tile(2),
                row(0), row(1), row(2),
                pl.BlockSpec((1, lw_dim), lambda i, c: (0, 0)),
                row(), row(), row(), row(), row(),
                low_w, low_w, low_w, low_w]
    return pl.pallas_call(
        functools.partial(_rwkv_prep_kernel, tiles_per_seq=seq_len // tm),
        out_shape=(out_sds,) * 7,
        grid=(t // tm, nc),
        in_specs=in_specs,
        out_specs=(out_spec,) * 7,
        scratch_shapes=[pltpu.VMEM((tm, lw_dim), bf16)] * 3,
        compiler_params=_params(("parallel", "arbitrary")),
        name="rwkv_prep",
    )(p_rkv, p_rkv, p_rkv, p_rkv, p_rkv, p_rkv, p_low, p_low, p_a,
      mu_full, mu_full, mu_full, mu_low,
      w0.reshape(1, -1), a0.reshape(1, -1), v0.reshape(1, -1), k_k.reshape(1, -1),
      k_a.reshape(1, -1), wup_pad, aup_pad, vup_pad, gup_pad)


def _rwkv_scan_kernel(r_ref, lw_ref, k_ref, v_ref, kk_ref, b_ref, g_ref, lng_ref, lnb_ref, rk_ref,
                      o_ref, state_ref):
    @pl.when(pl.program_id(2) == 0)
    def _():
        state_ref[...] = jnp.zeros_like(state_ref)

    c = SCAN_CHUNK
    n = 2 * c
    pairs = range(r_ref.shape[1] // LANES)
    cols = [slice(p * LANES, (p + 1) * LANES) for p in pairs]

    tri_r = lax.broadcasted_iota(jnp.int32, (c, c), 0)
    tri_c = lax.broadcasted_iota(jnp.int32, (c, c), 1)
    tri = jnp.where(tri_r >= tri_c, 1.0, 0.0)
    lw_all = lw_ref[...]
    lcum_all = jnp.dot(tri, lw_all, preferred_element_type=f32, precision=lax.Precision.HIGHEST)

    head0 = lax.broadcasted_iota(jnp.int32, (c, LANES), 1) < R_HEAD_DIM
    row = lax.broadcasted_iota(jnp.int32, (n, n), 0)
    col = lax.broadcasted_iota(jnp.int32, (n, n), 1)
    rin, cin = row & (c - 1), col & (c - 1)
    strict, incl = rin > cin, rin >= cin
    eye = jnp.where(row == col, 1.0, 0.0)

    def merge_mask(level):
        rb, cb = row >> level, col >> level
        return ((rb >> 1) == (cb >> 1)) & ((rb & 1) == 1) & ((cb & 1) == 0)

    def stack(x):
        return jnp.concatenate([jnp.where(head0, x, 0.0), jnp.where(head0, 0.0, x)], axis=0)

    ops = []
    for sl in cols:
        lcum, lw = lcum_all[:, sl], lw_all[:, sl]
        r, k, v, kk, b = r_ref[:, sl], k_ref[:, sl], v_ref[:, sl], kk_ref[:, sl], b_ref[:, sl]
        l_end = lcum[c - 1:c, :]
        e_in = jnp.exp(lcum)
        e_out = jnp.exp(-lcum)
        e_end = jnp.exp(l_end - lcum)
        ops.append(dict(
            a_t=stack(-kk * jnp.exp(lcum - lw)), r_t=stack(r * e_in),
            b_t=stack(b * e_out), k_t=stack(k * e_out),
            bk_h=jnp.concatenate([stack(b * e_end), stack(k * e_end)], axis=0),
            v_s=stack(v), decay=jnp.exp(l_end)))

    for o in ops:
        s1 = _dot_nt(jnp.concatenate([o["a_t"], o["r_t"]], axis=0),
                     jnp.concatenate([o["b_t"], o["k_t"]], axis=0))
        o["ab"] = jnp.where(strict, s1[:n, :n], 0.0)
        o["ak"] = jnp.where(strict, s1[:n, n:], 0.0)
        o["rbk"] = jnp.concatenate([jnp.where(incl, s1[n:, :n], 0.0),
                                    jnp.where(incl, s1[n:, n:], 0.0)], axis=1)

    mask = merge_mask(0)
    ts = [eye + jnp.where(mask, o["ab"], 0.0) for o in ops]
    for level in range(1, int(math.log2(c))):
        mask = merge_mask(level)
        zs = [_dot(jnp.where(mask, o["ab"], 0.0), t) for o, t in zip(ops, ts)]
        ts = [t + _dot(t, z) for t, z in zip(ts, zs)]

    states = [state_ref[p] for p in pairs]
    rhs = [_dot_nt(o["a_t"], s) + _dot(o["ak"], o["v_s"]) for o, s in zip(ops, states)]
    us = [_dot(t, x) for t, x in zip(ts, rhs)]
    uvs = [jnp.concatenate([u, o["v_s"]], axis=0) for u, o in zip(us, ops)]
    y_stacked = [_dot_nt(o["r_t"], s) + _dot(o["rbk"], uv) for o, s, uv in zip(ops, states, uvs)]
    for p, o, s, uv in zip(pairs, ops, states, uvs):
        state_ref[p] = s * o["decay"] + _dot_tn(uv, o["bk_h"])

    ones = _head_block_ones(LANES, R_HEAD_DIM)
    inv_n = 1.0 / R_HEAD_DIM
    ys = [y[:c, :] + y[c:, :] for y in y_stacked]
    mus = [_split_dot(y, ones) * inv_n for y in ys]
    ycs = [y - mu for y, mu in zip(ys, mus)]
    vars_ = [_split_dot(yc * yc, ones) * inv_n for yc in ycs]
    bonus = [_split_dot(r_ref[:, sl] * k_ref[:, sl] * rk_ref[:, sl], ones) for sl in cols]
    for sl, yc, var, bo in zip(cols, ycs, vars_, bonus):
        y = yc * lax.rsqrt(var + GN_EPS) * lng_ref[:, sl] + lnb_ref[:, sl]
        o_ref[:, sl] = ((y + bo * v_ref[:, sl]) * g_ref[:, sl]).astype(o_ref.dtype)


def rwkv_scan(r, lw, k, v, kk, b, g, ln_g, ln_b, r_k, batch, seq_len, *, wb=3072):
    t, mix_w = r.shape
    wb = _pick_tile(mix_w, wb)
    assert mix_w % wb == 0 and wb % LANES == 0 and seq_len % SCAN_CHUNK == 0
    nchunk = seq_len // SCAN_CHUNK
    tile = pl.BlockSpec((SCAN_CHUNK, wb), lambda bi, w, n: (bi * nchunk + n, w))
    row = pl.BlockSpec((1, wb), lambda bi, w, n: (0, w))
    return pl.pallas_call(
        _rwkv_scan_kernel,
        out_shape=jax.ShapeDtypeStruct((t, mix_w), bf16),
        grid=(batch, mix_w // wb, nchunk),
        in_specs=[tile] * 7 + [row] * 3,
        out_specs=tile,
        scratch_shapes=[pltpu.VMEM((wb // LANES, LANES, LANES), f32)],
        compiler_params=_params(("parallel", "parallel", "arbitrary")),
        name="rwkv_scan",
    )(r, lw, k, v, kk, b, g, ln_g.reshape(1, -1), ln_b.reshape(1, -1), r_k.reshape(1, -1))


def _pad_rows(w, start, total):
    return jnp.zeros((total, w.shape[1]), bf16).at[start:start + w.shape[0]].set(w.astype(bf16))


def kernel(x, mem, positions, mem_norm_g, mem_w_kv, a_norm_g, a_w_in, a_w_out, b_norm_g, b_w_in, b_shift_mu, b_w0, b_w_up, b_a0, b_a_up, b_v0, b_v_up, b_g_up, b_k_k, b_k_a, b_r_k, b_ln_g, b_ln_b, b_w_out, ffn_norm_g, ffn_w_up, ffn_conv_w, ffn_conv_b, ffn_w_down, final_norm_g):
    batch, seq_len, d_model = x.shape
    t = batch * seq_len
    depth = ffn_w_up.shape[0]
    xw = d_model // 4
    mix_w = d_model - xw
    x = x.reshape(t, d_model)

    mem_h = rmsnorm(mem.reshape(-1, d_model), mem_norm_g, bf16)
    mem_kv = matmul(mem_h, mem_w_kv[None], 0)
    tabs = rope_tables(positions)

    p_a = None
    for i in range(depth):
        j = i // 2
        if i % 2 == 0:
            h = rmsnorm(x, a_norm_g[j], bf16)
            p = matmul(h, a_w_in, j)
            if p_a is None:
                p_a = p
            mix = dilated_attn(p, tabs, batch, seq_len)
            xo = cross_attn(p, 3 * mix_w // xw, mem_kv, batch, seq_len)
            w_out = a_w_out
        else:
            h = rmsnorm(x, b_norm_g[j], bf16)
            low_w = b_w_in.shape[2] - 3 * mix_w - xw
            w_b = b_w_in[j]
            p_rkv = matmul(h, w_b[None, :, :3 * mix_w], 0)
            p_low = matmul(h, w_b[None, :, 3 * mix_w:3 * mix_w + low_w], 0)
            cq = matmul(h, w_b[None, :, 3 * mix_w + low_w:], 0, out_dtype=bf16)
            dd, da, dv = b_w_up.shape[1], b_a_up.shape[1], b_v_up.shape[1]
            prep = rwkv_prep(
                p_rkv, p_low, p_a, mix_w, seq_len, b_shift_mu[j], b_w0[j], b_a0[j], b_v0[j],
                b_k_k[j], b_k_a[j],
                _pad_rows(b_w_up[j], 0, low_w), _pad_rows(b_a_up[j], dd, low_w),
                _pad_rows(b_v_up[j], dd + da, low_w), _pad_rows(b_g_up[j], dd + da + dv, low_w))
            mix = rwkv_scan(*prep, b_ln_g[j], b_ln_b[j], b_r_k[j], batch, seq_len)
            xo = cross_attn(cq, 0, mem_kv, batch, seq_len)
            w_out = b_w_out
        x = out_proj(mix, xo, w_out, j, x)
        h = rmsnorm(x, ffn_norm_g[i], bf16)
        act = ffn_up(h, ffn_w_up, ffn_conv_w, ffn_conv_b, i, seq_len)
        x = matmul(act, ffn_w_down, i, res=x, tn=1024, tk=2048)
    return rmsnorm(x, final_norm_g, f32).reshape(batch, seq_len, d_model)
```

```python
import functools
import math

import jax
import jax.numpy as jnp
from jax import lax
from jax.experimental import pallas as pl
from jax.experimental.pallas import tpu as pltpu

f32 = jnp.float32
bf16 = jnp.bfloat16

X_HEADS = 4
A_HEAD_DIM = 128
A_GROUPS = ((128, 1), (512, 4), (2048, 16))
ATT_BLOCK = 128
ATT_UNITS_PER_BATCH = 8
ROT_DIM = A_HEAD_DIM // 4
ROPE_THETA = 500000.0
R_HEAD_DIM = 64
CONV_WIDTH = 3
NORM_EPS = 1e-6
GN_EPS = 64e-5
L2_EPS = 1e-12
NEG_INF = -1e30

V7X_VMEM_BYTES = 64 * 2**20
VMEM_LIMIT_BYTES = V7X_VMEM_BYTES - 4 * 2**20
SUBLANES = 8
LANES = 128

SCAN_CHUNK = LANES // 2
HEADS_PER_LANE_GROUP = LANES // R_HEAD_DIM


def _pick_tile(n, preferred):
    if n % preferred == 0:
        return preferred
    g = math.gcd(n, preferred)
    return g if g % LANES == 0 else n


def _params(semantics):
    return pltpu.CompilerParams(dimension_semantics=semantics,
                                vmem_limit_bytes=VMEM_LIMIT_BYTES)


def _dot(a, b):
    return jnp.dot(a.astype(bf16), b.astype(bf16), preferred_element_type=f32)


def _dot_nt(a, b):
    return lax.dot_general(a.astype(bf16), b.astype(bf16), (((1,), (1,)), ((), ())),
                           preferred_element_type=f32)


def _dot_tn(a, b):
    return lax.dot_general(a.astype(bf16), b.astype(bf16), (((0,), (0,)), ((), ())),
                           preferred_element_type=f32)


def _sigmoid(x):
    return 1.0 / (1.0 + jnp.exp(-x))


def _split_dot(x, ones_mat):
    hi = x.astype(bf16)
    lo = (x - hi.astype(f32)).astype(bf16)
    return (jnp.dot(hi, ones_mat, preferred_element_type=f32)
            + jnp.dot(lo, ones_mat, preferred_element_type=f32))


def _head_block_ones(width, head_dim):
    shift = int(math.log2(head_dim))
    r = lax.broadcasted_iota(jnp.int32, (width, width), 0) >> shift
    c = lax.broadcasted_iota(jnp.int32, (width, width), 1) >> shift
    return jnp.where(r == c, 1.0, 0.0).astype(bf16)


def _rmsnorm_kernel(x_ref, g_ref, o_ref):
    x = x_ref[...].astype(f32)
    ms = jnp.mean(x * x, axis=-1, keepdims=True)
    o_ref[...] = (x * lax.rsqrt(ms + NORM_EPS) * g_ref[...]).astype(o_ref.dtype)


def rmsnorm(x, g, out_dtype, tm=256):
    m, d = x.shape
    tm = min(tm, m)
    return pl.pallas_call(
        _rmsnorm_kernel,
        out_shape=jax.ShapeDtypeStruct((m, d), out_dtype),
        grid=(m // tm,),
        in_specs=[pl.BlockSpec((tm, d), lambda i: (i, 0)),
                  pl.BlockSpec((1, d), lambda i: (0, 0))],
        out_specs=pl.BlockSpec((tm, d), lambda i: (i, 0)),
        compiler_params=_params(("parallel",)),
        name="rmsnorm",
    )(x, g.reshape(1, d))


def _store_normed(x, g_ref, hn_ref, ss_ref):
    hn_ref[...] = (x * g_ref[...]).astype(hn_ref.dtype)
    ss_ref[...] = jnp.sum(x * x, axis=-1, keepdims=True)


def _row_rstd(ss_ref, width):
    return lax.rsqrt(jnp.sum(ss_ref[...], axis=0) * (1.0 / width) + NORM_EPS)


def _combine_ss_kernel(ss_ref, o_ref):
    o_ref[...] = jnp.sum(ss_ref[...], axis=0, keepdims=True)


def combine_row_ss(ss, tm=1024):
    nj, m, _ = ss.shape
    tm = _pick_tile(m, tm)
    return pl.pallas_call(
        _combine_ss_kernel,
        out_shape=jax.ShapeDtypeStruct((1, m, 1), f32),
        grid=(m // tm,),
        in_specs=[pl.BlockSpec((nj, tm, 1), lambda i: (0, i, 0))],
        out_specs=pl.BlockSpec((1, tm, 1), lambda i: (0, i, 0)),
        compiler_params=_params(("parallel",)),
        name="combine_row_ss",
    )(ss)


def _normed_outputs(m, n, tm, tn, index_map_2d, index_map_ss):
    shapes = [jax.ShapeDtypeStruct((m, n), bf16), jax.ShapeDtypeStruct((n // tn, m, 1), f32)]
    specs = [pl.BlockSpec((tm, tn), index_map_2d), pl.BlockSpec((None, tm, 1), index_map_ss)]
    return shapes, specs


def _matmul_kernel(*refs, nk, has_res, has_row_ss, emits_normed, w_is_transposed):
    a_ref, w_ref = refs[:2]
    rest = list(refs[2:])
    res_ref = rest.pop(0) if has_res else None
    row_ss_ref = rest.pop(0) if has_row_ss else None
    gain_ref = rest.pop(0) if emits_normed else None
    o_ref = rest.pop(0)
    hn_ref, ss_ref = (rest.pop(0), rest.pop(0)) if emits_normed else (None, None)
    dot = _dot_nt if w_is_transposed else _dot

    def finish(out):
        o_ref[...] = out.astype(o_ref.dtype)
        if emits_normed:
            _store_normed(out, gain_ref, hn_ref, ss_ref)

    if nk == 1:
        out = dot(a_ref[...], w_ref[...])
        if has_row_ss:
            out = out * _row_rstd(row_ss_ref, a_ref.shape[1])
        if has_res:
            out = out + res_ref[...]
        finish(out)
        return
    acc_ref, = rest
    k = pl.program_id(2)

    @pl.when(k == 0)
    def _():
        acc_ref[...] = res_ref[...] if has_res else jnp.zeros_like(acc_ref)

    acc_ref[...] += dot(a_ref[...], w_ref[...])

    @pl.when(k == nk - 1)
    def _():
        finish(acc_ref[...])


def matmul(a, w, layer, *, w_is_transposed=False, n=None, res=None, row_ss=None, norm_gain=None,
           out_dtype=f32, tm=1024, tn=512, tk=None):
    m, k_dim = a.shape
    if n is None:
        n = w.shape[1] if w_is_transposed else w.shape[2]
    tm, tn = _pick_tile(m, tm), _pick_tile(n, tn)
    tk = k_dim if tk is None else _pick_tile(k_dim, tk)
    assert m % tm == 0 and n % tn == 0 and k_dim % tk == 0, (a.shape, w.shape, tm, tn, tk)
    nk = k_dim // tk
    assert row_ss is None or nk == 1
    if w_is_transposed:
        w_spec = pl.BlockSpec((None, tn, tk), lambda i, j, k: (layer, j, k))
    else:
        w_spec = pl.BlockSpec((None, tk, tn), lambda i, j, k: (layer, k, j))
    in_specs = [pl.BlockSpec((tm, tk), lambda i, j, k: (i, k)), w_spec]
    args = [a, w]
    if res is not None:
        in_specs.append(pl.BlockSpec((tm, tn), lambda i, j, k: (i, j)))
        args.append(res)
    if row_ss is not None:
        in_specs.append(pl.BlockSpec((row_ss.shape[0], tm, 1), lambda i, j, k: (0, i, 0)))
        args.append(row_ss)
    out_shape = [jax.ShapeDtypeStruct((m, n), out_dtype)]
    out_specs = [pl.BlockSpec((tm, tn), lambda i, j, k: (i, j))]
    if norm_gain is not None:
        in_specs.append(pl.BlockSpec((1, tn), lambda i, j, k: (0, j)))
        args.append(norm_gain.reshape(1, n))
        shapes, specs = _normed_outputs(m, n, tm, tn, lambda i, j, k: (i, j), lambda i, j, k: (j, i, 0))
        out_shape += shapes
        out_specs += specs
    outs = pl.pallas_call(
        functools.partial(_matmul_kernel, nk=nk, has_res=res is not None,
                          has_row_ss=row_ss is not None, emits_normed=norm_gain is not None,
                          w_is_transposed=w_is_transposed),
        out_shape=out_shape,
        grid=(m // tm, n // tn, nk),
        in_specs=in_specs,
        out_specs=out_specs,
        scratch_shapes=[pltpu.VMEM((tm, tn), f32)] if nk > 1 else [],
        compiler_params=_params(("parallel", "parallel", "arbitrary")),
        name="matmul",
    )(*args)
    return outs[0] if norm_gain is None else (outs[0], (outs[1], combine_row_ss(outs[2])))


def _out_proj_kernel(mix_ref, xo_ref, w_ref, res_ref, gain_ref, o_ref, hn_ref, ss_ref):
    k_mix = mix_ref.shape[1]
    out = (res_ref[...] + _dot(mix_ref[...], w_ref[:k_mix, :])
           + _dot(xo_ref[...], w_ref[k_mix:, :]))
    o_ref[...] = out
    _store_normed(out, gain_ref, hn_ref, ss_ref)


def out_proj(mix, xo, w, layer, res, norm_gain, *, tm=1024, tn=512):
    m, k_mix = mix.shape
    k_xo = xo.shape[1]
    n = w.shape[2]
    tm, tn = _pick_tile(m, tm), _pick_tile(n, tn)
    assert w.shape[1] == k_mix + k_xo
    shapes, specs = _normed_outputs(m, n, tm, tn, lambda i, j: (i, j), lambda i, j: (j, i, 0))
    x, hn, ss = pl.pallas_call(
        _out_proj_kernel,
        out_shape=[jax.ShapeDtypeStruct((m, n), f32)] + shapes,
        grid=(m // tm, n // tn),
        in_specs=[pl.BlockSpec((tm, k_mix), lambda i, j: (i, 0)),
                  pl.BlockSpec((tm, k_xo), lambda i, j: (i, 0)),
                  pl.BlockSpec((None, k_mix + k_xo, tn), lambda i, j: (layer, 0, j)),
                  pl.BlockSpec((tm, tn), lambda i, j: (i, j)),
                  pl.BlockSpec((1, tn), lambda i, j: (0, j))],
        out_specs=[pl.BlockSpec((tm, tn), lambda i, j: (i, j))] + specs,
        compiler_params=_params(("parallel", "parallel")),
        name="out_proj",
    )(mix, xo, w, res, norm_gain.reshape(1, n))
    return x, (hn, combine_row_ss(ss))


def _ffn_up_kernel(a_ref, ss_ref, wg_ref, wu_ref, cw_ref, cb_ref, o_ref, wg_bf, wu_bf, tail_ref,
                   *, tiles_per_seq):
    i = pl.program_id(1)

    @pl.when(i == 0)
    def _():
        wg_bf[...] = wg_ref[...].astype(bf16)
        wu_bf[...] = wu_ref[...].astype(bf16)

    a = a_ref[...]
    rstd = _row_rstd(ss_ref, a.shape[1])
    gate = jnp.dot(a, wg_bf[...], preferred_element_type=f32) * rstd
    up = jnp.dot(a, wu_bf[...], preferred_element_type=f32) * rstd
    tm = gate.shape[0]

    tail = jnp.where(i % tiles_per_seq == 0, 0.0, tail_ref[...])
    tail_ref[...] = gate[tm - SUBLANES:, :]

    row = lax.broadcasted_iota(jnp.int32, gate.shape, 0)
    last, last2 = tail[SUBLANES - 1:SUBLANES, :], tail[SUBLANES - 2:SUBLANES - 1, :]
    g1 = jnp.where(row == 0, last, pltpu.roll(gate, 1, 0))
    g2 = jnp.where(row == 0, last2, jnp.where(row == 1, last, pltpu.roll(gate, 2, 0)))
    cw = cw_ref[...]
    conv = cb_ref[...] + cw[0:1, :] * g2 + cw[1:2, :] * g1 + cw[2:3, :] * gate
    o_ref[...] = (conv * _sigmoid(conv) * up).astype(o_ref.dtype)


def ffn_up(normed, w_up, conv_w, conv_b, layer, seq_len, *, tm=512, tn=512):
    h, row_ss = normed
    m, d = h.shape
    d_ff = w_up.shape[2] // 2
    tm, tn = _pick_tile(seq_len, tm), _pick_tile(d_ff, tn)
    assert m % tm == 0 and seq_len % tm == 0 and d_ff % tn == 0
    nj = d_ff // tn
    depth = conv_w.shape[0]
    return pl.pallas_call(
        functools.partial(_ffn_up_kernel, tiles_per_seq=seq_len // tm),
        out_shape=jax.ShapeDtypeStruct((m, d_ff), bf16),
        grid=(nj, m // tm),
        in_specs=[pl.BlockSpec((tm, d), lambda j, i: (i, 0)),
                  pl.BlockSpec((row_ss.shape[0], tm, 1), lambda j, i: (0, i, 0)),
                  pl.BlockSpec((None, d, tn), lambda j, i: (layer, 0, j)),
                  pl.BlockSpec((None, d, tn), lambda j, i: (layer, 0, j + nj)),
                  pl.BlockSpec((None, CONV_WIDTH, tn), lambda j, i: (layer, 0, j)),
                  pl.BlockSpec((None, 1, tn), lambda j, i: (layer, 0, j))],
        out_specs=pl.BlockSpec((tm, tn), lambda j, i: (i, j)),
        scratch_shapes=[pltpu.VMEM((d, tn), bf16), pltpu.VMEM((d, tn), bf16),
                        pltpu.VMEM((SUBLANES, tn), f32)],
        compiler_params=_params(("arbitrary", "arbitrary")),
        name="ffn_up",
    )(h, row_ss, w_up, w_up, conv_w, conv_b.reshape(depth, 1, d_ff))


def _rope_table_kernel(pos_ref, freq_ref, o_ref):
    ang = pos_ref[...].astype(f32) * freq_ref[...]
    lane = lax.broadcasted_iota(jnp.int32, ang.shape, 1)
    o_ref[0] = jnp.where(lane < ROT_DIM, jnp.cos(ang), 1.0)
    o_ref[1] = jnp.where(lane < ROT_DIM, jnp.sin(ang), 0.0)


def rope_tables(positions, tm=512):
    t = positions.size
    half = ROT_DIM // 2
    inv_freq = ROPE_THETA ** (-jnp.arange(half, dtype=f32) / half)
    freq = jnp.zeros((1, A_HEAD_DIM), f32).at[0, :ROT_DIM].set(jnp.tile(inv_freq, 2))
    tm = _pick_tile(t, tm)
    return pl.pallas_call(
        _rope_table_kernel,
        out_shape=jax.ShapeDtypeStruct((2, t, LANES), f32),
        grid=(t // tm,),
        in_specs=[pl.BlockSpec((tm, 1), lambda i: (i, 0)),
                  pl.BlockSpec((1, A_HEAD_DIM), lambda i: (0, 0))],
        out_specs=pl.BlockSpec((2, tm, LANES), lambda i: (0, i, 0)),
        compiler_params=_params(("parallel",)),
        name="rope_tables",
    )(positions.reshape(t, 1), freq)


def _rotate_half_matrix():
    half = ROT_DIM // 2
    src = lax.broadcasted_iota(jnp.int32, (A_HEAD_DIM, A_HEAD_DIM), 0)
    dst = lax.broadcasted_iota(jnp.int32, (A_HEAD_DIM, A_HEAD_DIM), 1)
    minus = (dst < half) & (src == dst + half)
    plus = (dst >= half) & (dst < ROT_DIM) & (src == dst - half)
    return jnp.where(minus, -1.0, jnp.where(plus, 1.0, 0.0)).astype(bf16)


def _rope(x, cos, sin, rot_mat):
    return x * cos + _split_dot(x, rot_mat) * sin


def _dilated_attn_kernel(*refs, dilations, n_backs):
    ng = len(dilations)
    q_refs, kc_refs, kp_refs = refs[0:ng], refs[ng:2 * ng], refs[2 * ng:3 * ng]
    vc_refs, vp_refs = refs[3 * ng:4 * ng], refs[4 * ng:5 * ng]
    tc_ref, tp_ref, o_ref = refs[5 * ng:5 * ng + 3]
    scratch = refs[5 * ng + 3:]
    o_scr, lse_scr, q_scr, k_scr, kp_scr = (scratch[i * ng:(i + 1) * ng] for i in range(5))

    blk = ATT_BLOCK
    span = q_refs[0].shape[0]
    first_span = pl.program_id(1) == 0
    qi = lax.broadcasted_iota(jnp.int32, (blk, 2 * blk), 0)
    col = lax.broadcasted_iota(jnp.int32, (blk, 2 * blk), 1)
    dist = qi + blk - col
    first_col = jnp.where(first_span, blk, 0)
    scale = A_HEAD_DIM ** -0.5

    cos_c, sin_c = tc_ref[0], tc_ref[1]
    rot_mat = _rotate_half_matrix()
    for g, d in enumerate(dilations):
        tail = pl.ds(span - d * blk, d * blk)
        q_scr[g][...] = _rope(q_refs[g][...], cos_c, sin_c, rot_mat) * scale
        k_scr[g][...] = _rope(kc_refs[g][...], cos_c, sin_c, rot_mat)
        kp_scr[g][...] = _rope(kp_refs[g][...], tp_ref[0, tail, :], tp_ref[1, tail, :], rot_mat)

    ones_kv = jnp.ones((2 * blk, A_HEAD_DIM), bf16)

    def rows(start, size, d):
        return pl.ds(start, size, stride=d) if d > 1 else pl.ds(start, size)

    def load_unit(g, d, r, j):
        span_g = d * blk
        cur = rows(j * span_g + r, blk, d)
        if j == 0:
            prev = rows(r, blk, d)
            k2 = jnp.concatenate([kp_scr[g][prev, :], k_scr[g][cur, :]], axis=0)
            v2 = jnp.concatenate([vp_refs[g][prev, :], vc_refs[g][cur, :]], axis=0)
        else:
            both = rows((j - 1) * span_g + r, 2 * blk, d)
            k2 = k_scr[g][both, :]
            v2 = vc_refs[g][both, :]
        return q_scr[g][cur, :].astype(bf16), k2.astype(bf16), v2.astype(bf16)

    for g, (d, n_back) in enumerate(zip(dilations, n_backs)):
        in_window = (dist >= 0) & (dist <= n_back)
        in_window_first = in_window & (col >= first_col)
        units = [(r, j) for r in range(d) for j in range(span // (d * blk))]
        for b0 in range(0, len(units), ATT_UNITS_PER_BATCH):
            batch_units = units[b0:b0 + ATT_UNITS_PER_BATCH]
            loaded = [load_unit(g, d, r, j) for r, j in batch_units]
            scores = [jnp.where(in_window_first if j == 0 else in_window, _dot_nt(q, k2), NEG_INF)
                      for (q, k2, _), (_, j) in zip(loaded, batch_units)]
            maxes = [jnp.max(s, axis=-1, keepdims=True) for s in scores]
            probs = [jnp.exp(s - m).astype(bf16) for s, m in zip(scores, maxes)]
            sums = [jnp.dot(p, ones_kv, preferred_element_type=f32) for p in probs]
            outs = [_dot(p, v2) * (1.0 / l) for p, (_, _, v2), l in zip(probs, loaded, sums)]
            for (r, j), o, m, l in zip(batch_units, outs, maxes, sums):
                cur = rows(j * d * blk + r, blk, d)
                o_scr[g][cur, :] = o
                lse_scr[g][cur, :] = m + jnp.log(l)

    lses = [ref[...] for ref in lse_scr]
    m = functools.reduce(jnp.maximum, lses)
    wts = [jnp.exp(x - m) for x in lses]
    num = sum(w * ref[...] for w, ref in zip(wts, o_scr))
    o_ref[...] = (num / sum(wts)).astype(o_ref.dtype)


def dilated_attn(p_in, tabs, batch, seq_len):
    t, ncol = p_in.shape
    ng = len(A_GROUPS)
    gw = ncol // (3 * ng + 1)
    heads = gw // A_HEAD_DIM
    dilations = tuple(d for _, d in A_GROUPS)
    n_backs = tuple(w // d for w, d in A_GROUPS)
    span = max(dilations) * ATT_BLOCK
    assert all(nb <= ATT_BLOCK for nb in n_backs) and seq_len % span == 0
    assert all(span % (d * ATT_BLOCK) == 0 for d in dilations)
    nspan = seq_len // span

    def cur(part, g):
        return pl.BlockSpec((span, A_HEAD_DIM),
                            lambda b, n, h: (b * nspan + n, (part * ng + g) * heads + h))

    def prev(part, g):
        per_span = span // (dilations[g] * ATT_BLOCK)
        return pl.BlockSpec(
            (dilations[g] * ATT_BLOCK, A_HEAD_DIM),
            lambda b, n, h: (jnp.maximum((b * nspan + n) * per_span - 1, 0), (part * ng + g) * heads + h))

    groups = range(ng)
    in_specs = ([cur(0, g) for g in groups] + [cur(1, g) for g in groups] + [prev(1, g) for g in groups]
                + [cur(2, g) for g in groups] + [prev(2, g) for g in groups]
                + [pl.BlockSpec((2, span, LANES), lambda b, n, h: (0, b * nspan + n, 0)),
                   pl.BlockSpec((2, span, LANES), lambda b, n, h: (0, jnp.maximum(b * nspan + n - 1, 0), 0))])
    return pl.pallas_call(
        functools.partial(_dilated_attn_kernel, dilations=dilations, n_backs=n_backs),
        out_shape=jax.ShapeDtypeStruct((t, gw), bf16),
        grid=(batch, nspan, heads),
        in_specs=in_specs,
        out_specs=pl.BlockSpec((span, A_HEAD_DIM), lambda b, n, h: (b * nspan + n, h)),
        scratch_shapes=([pltpu.VMEM((span, A_HEAD_DIM), f32)] * (4 * ng)
                        + [pltpu.VMEM((d * ATT_BLOCK, A_HEAD_DIM), f32) for d in dilations]),
        compiler_params=_params(("parallel", "parallel", "arbitrary")),
        name="dilated_attn",
    )(*([p_in] * (5 * ng)), tabs, tabs)


def _xattn_kernel(q_ref, k_ref, v_ref, o_ref, *, heads):
    hd = q_ref.shape[1] // heads
    scale = hd ** -0.5
    for h in range(heads):
        sl = slice(h * hd, (h + 1) * hd)
        s = _dot_nt(q_ref[:, sl], k_ref[:, sl]) * scale
        m = jnp.max(s, axis=-1, keepdims=True)
        p = jnp.exp(s - m)
        l = jnp.sum(p, axis=-1, keepdims=True)
        o_ref[:, sl] = _dot(p / l, v_ref[:, sl]).astype(o_ref.dtype)


def cross_attn(q_arr, q_col_block, mem_kv, batch, seq_len, tq=512):
    xw = mem_kv.shape[1] // 2
    mem_len = mem_kv.shape[0] // batch
    tq = _pick_tile(seq_len, tq)
    nq = seq_len // tq
    return pl.pallas_call(
        functools.partial(_xattn_kernel, heads=X_HEADS),
        out_shape=jax.ShapeDtypeStruct((batch * seq_len, xw), bf16),
        grid=(batch, nq),
        in_specs=[pl.BlockSpec((tq, xw), lambda b, i: (b * nq + i, q_col_block)),
                  pl.BlockSpec((mem_len, xw), lambda b, i: (b, 0)),
                  pl.BlockSpec((mem_len, xw), lambda b, i: (b, 1))],
        out_specs=pl.BlockSpec((tq, xw), lambda b, i: (b * nq + i, 0)),
        compiler_params=_params(("parallel", "parallel")),
        name="cross_attn",
    )(q_arr, mem_kv, mem_kv)


def _shift(x, halo, mu, seq_start):
    row = lax.broadcasted_iota(jnp.int32, x.shape, 0)
    last = jnp.where(seq_start, 0.0, halo[SUBLANES - 1:SUBLANES, :])
    prev = jnp.where(row == 0, last, pltpu.roll(x, 1, 0))
    return x + (prev - x) * mu


def _rwkv_prep_kernel(pr, pk, pv, hr, hk, hv, lo, hlo, vf,
                      mu_r, mu_k, mu_v, mu_l, w0, a0, v0, k_k, k_a, wup, aup, vup, gup,
                      r_o, lw_o, k_o, v_o, kk_o, b_o, g_o, low_s, low_tanh_s, low_sig_s,
                      *, tiles_per_seq):
    seq_start = pl.program_id(0) % tiles_per_seq == 0

    @pl.when(pl.program_id(1) == 0)
    def _():
        low = _shift(lo[...], hlo[...], mu_l[...], seq_start)
        low_s[...] = low.astype(bf16)
        low_tanh_s[...] = jnp.tanh(low).astype(bf16)
        low_sig_s[...] = _sigmoid(low).astype(bf16)

    r = _shift(pr[...], hr[...], mu_r[...], seq_start)
    k = _shift(pk[...], hk[...], mu_k[...], seq_start)
    v = _shift(pv[...], hv[...], mu_v[...], seq_start)

    low = low_s[...]
    xw = w0[...] + _dot(low_tanh_s[...], wup[...])
    lw_o[...] = -math.exp(-0.5) * _sigmoid(xw)
    ag = _sigmoid(a0[...] + _dot(low, aup[...]))
    v_gate = _sigmoid(v0[...] + _dot(low, vup[...]))
    g_o[...] = _dot(low_sig_s[...], gup[...])
    v_o[...] = v + (vf[...] - v) * v_gate
    r_o[...] = r

    kk = k * k_k[...]
    ones = _head_block_ones(kk.shape[1], R_HEAD_DIM)
    norm = jnp.sqrt(_split_dot(kk * kk, ones))
    kk = kk / jnp.maximum(norm, L2_EPS)
    kk_o[...] = kk
    b_o[...] = kk * ag
    k_o[...] = k * (1.0 + (ag - 1.0) * k_a[...])


def rwkv_prep(p_rkv, p_low, p_a, mix_w, seq_len, mu, w0, a0, v0, k_k, k_a,
              wup_pad, aup_pad, vup_pad, gup_pad, *, tm=512, tc=512):
    t = p_rkv.shape[0]
    lw_dim = p_low.shape[1]
    tm, tc = _pick_tile(seq_len, tm), _pick_tile(mix_w, tc)
    assert seq_len % tm == 0 and mix_w % tc == 0 and tc % LANES == 0
    nc = mix_w // tc
    hb = tm // SUBLANES

    def tile(cb):
        return pl.BlockSpec((tm, tc), lambda i, c: (i, cb * nc + c))

    def halo(cb):
        return pl.BlockSpec((SUBLANES, tc), lambda i, c: (jnp.maximum(i * hb - 1, 0), cb * nc + c))

    def row(cb=0):
        return pl.BlockSpec((1, tc), lambda i, c: (0, cb * nc + c))

    low_w = pl.BlockSpec((lw_dim, tc), lambda i, c: (0, c))
    out_spec = pl.BlockSpec((tm, tc), lambda i, c: (i, c))
    out_sds = jax.ShapeDtypeStruct((t, mix_w), f32)
    mu_full = mu.reshape(1, -1)
    mu_low = mu[3 * mix_w:].reshape(1, lw_dim)
    in_specs = [tile(0), tile(1), tile(2), halo(0), halo(1), halo(2),
                pl.BlockSpec((tm, lw_dim), lambda i, c: (i, 0)),
                pl.BlockSpec((SUBLANES, lw_dim), lambda i, c: (jnp.maximum(i * hb - 1, 0), 0)),
                tile(2),
                row(0), row(1), row(2),
                pl.BlockSpec((1, lw_dim), lambda i, c: (0, 0)),
                row(), row(), row(), row(), row(),
                low_w, low_w, low_w, low_w]
    return pl.pallas_call(
        functools.partial(_rwkv_prep_kernel, tiles_per_seq=seq_len // tm),
        out_shape=(out_sds,) * 7,
        grid=(t // tm, nc),
        in_specs=in_specs,
        out_specs=(out_spec,) * 7,
        scratch_shapes=[pltpu.VMEM((tm, lw_dim), bf16)] * 3,
        compiler_params=_params(("parallel", "arbitrary")),
        name="rwkv_prep",
    )(p_rkv, p_rkv, p_rkv, p_rkv, p_rkv, p_rkv, p_low, p_low, p_a,
      mu_full, mu_full, mu_full, mu_low,
      w0.reshape(1, -1), a0.reshape(1, -1), v0.reshape(1, -1), k_k.reshape(1, -1),
      k_a.reshape(1, -1), wup_pad, aup_pad, vup_pad, gup_pad)


def _rwkv_scan_kernel(r_ref, lw_ref, k_ref, v_ref, kk_ref, b_ref, g_ref, lng_ref, lnb_ref, rk_ref,
                      o_ref, state_ref):
    @pl.when(pl.program_id(2) == 0)
    def _():
        state_ref[...] = jnp.zeros_like(state_ref)

    c = SCAN_CHUNK
    n = 2 * c
    pairs = range(r_ref.shape[1] // LANES)
    cols = [slice(p * LANES, (p + 1) * LANES) for p in pairs]

    tri_r = lax.broadcasted_iota(jnp.int32, (c, c), 0)
    tri_c = lax.broadcasted_iota(jnp.int32, (c, c), 1)
    tri = jnp.where(tri_r >= tri_c, 1.0, 0.0)
    lw_all = lw_ref[...]
    lcum_all = jnp.dot(tri, lw_all, preferred_element_type=f32, precision=lax.Precision.HIGHEST)

    head0 = lax.broadcasted_iota(jnp.int32, (c, LANES), 1) < R_HEAD_DIM
    row = lax.broadcasted_iota(jnp.int32, (n, n), 0)
    col = lax.broadcasted_iota(jnp.int32, (n, n), 1)
    rin, cin = row & (c - 1), col & (c - 1)
    strict, incl = rin > cin, rin >= cin
    eye = jnp.where(row == col, 1.0, 0.0)

    def merge_mask(level):
        rb, cb = row >> level, col >> level
        return ((rb >> 1) == (cb >> 1)) & ((rb & 1) == 1) & ((cb & 1) == 0)

    def stack(x):
        return jnp.concatenate([jnp.where(head0, x, 0.0), jnp.where(head0, 0.0, x)], axis=0)

    ops = []
    for sl in cols:
        lcum, lw = lcum_all[:, sl], lw_all[:, sl]
        r, k, v, kk, b = r_ref[:, sl], k_ref[:, sl], v_ref[:, sl], kk_ref[:, sl], b_ref[:, sl]
        l_end = lcum[c - 1:c, :]
        e_in = jnp.exp(lcum)
        e_out = jnp.exp(-lcum)
        e_end = jnp.exp(l_end - lcum)
        ops.append(dict(
            a_t=stack(-kk * jnp.exp(lcum - lw)), r_t=stack(r * e_in),
            b_t=stack(b * e_out), k_t=stack(k * e_out),
            bk_h=jnp.concatenate([stack(b * e_end), stack(k * e_end)], axis=0),
            v_s=stack(v), decay=jnp.exp(l_end)))

    for o in ops:
        s1 = _dot_nt(jnp.concatenate([o["a_t"], o["r_t"]], axis=0),
                     jnp.concatenate([o["b_t"], o["k_t"]], axis=0))
        o["ab"] = jnp.where(strict, s1[:n, :n], 0.0)
        o["ak"] = jnp.where(strict, s1[:n, n:], 0.0)
        o["rbk"] = jnp.concatenate([jnp.where(incl, s1[n:, :n], 0.0),
                                    jnp.where(incl, s1[n:, n:], 0.0)], axis=1)

    mask = merge_mask(0)
    ts = [eye + jnp.where(mask, o["ab"], 0.0) for o in ops]
    for level in range(1, int(math.log2(c))):
        mask = merge_mask(level)
        zs = [_dot(jnp.where(mask, o["ab"], 0.0), t) for o, t in zip(ops, ts)]
        ts = [t + _dot(t, z) for t, z in zip(ts, zs)]

    states = [state_ref[p] for p in pairs]
    rhs = [_dot_nt(o["a_t"], s) + _dot(o["ak"], o["v_s"]) for o, s in zip(ops, states)]
    us = [_dot(t, x) for t, x in zip(ts, rhs)]
    uvs = [jnp.concatenate([u, o["v_s"]], axis=0) for u, o in zip(us, ops)]
    y_stacked = [_dot_nt(o["r_t"], s) + _dot(o["rbk"], uv) for o, s, uv in zip(ops, states, uvs)]
    for p, o, s, uv in zip(pairs, ops, states, uvs):
        state_ref[p] = s * o["decay"] + _dot_tn(uv, o["bk_h"])

    ones = _head_block_ones(LANES, R_HEAD_DIM)
    inv_n = 1.0 / R_HEAD_DIM
    ys = [y[:c, :] + y[c:, :] for y in y_stacked]
    mus = [_split_dot(y, ones) * inv_n for y in ys]
    ycs = [y - mu for y, mu in zip(ys, mus)]
    vars_ = [_split_dot(yc * yc, ones) * inv_n for yc in ycs]
    bonus = [_split_dot(r_ref[:, sl] * k_ref[:, sl] * rk_ref[:, sl], ones) for sl in cols]
    for sl, yc, var, bo in zip(cols, ycs, vars_, bonus):
        y = yc * lax.rsqrt(var + GN_EPS) * lng_ref[:, sl] + lnb_ref[:, sl]
        o_ref[:, sl] = ((y + bo * v_ref[:, sl]) * g_ref[:, sl]).astype(o_ref.dtype)


def rwkv_scan(r, lw, k, v, kk, b, g, ln_g, ln_b, r_k, batch, seq_len, *, wb=3072):
    t, mix_w = r.shape
    wb = _pick_tile(mix_w, wb)
    assert mix_w % wb == 0 and wb % LANES == 0 and seq_len % SCAN_CHUNK == 0
    nchunk = seq_len // SCAN_CHUNK
    tile = pl.BlockSpec((SCAN_CHUNK, wb), lambda bi, w, n: (bi * nchunk + n, w))
    row = pl.BlockSpec((1, wb), lambda bi, w, n: (0, w))
    return pl.pallas_call(
        _rwkv_scan_kernel,
        out_shape=jax.ShapeDtypeStruct((t, mix_w), bf16),
        grid=(batch, mix_w // wb, nchunk),
        in_specs=[tile] * 7 + [row] * 3,
        out_specs=tile,
        scratch_shapes=[pltpu.VMEM((wb // LANES, LANES, LANES), f32)],
        compiler_params=_params(("parallel", "parallel", "arbitrary")),
        name="rwkv_scan",
    )(r, lw, k, v, kk, b, g, ln_g.reshape(1, -1), ln_b.reshape(1, -1), r_k.reshape(1, -1))


def _pad_rows(w, start, total):
    return jnp.zeros((total, w.shape[1]), bf16).at[start:start + w.shape[0]].set(w.astype(bf16))


def kernel(x, mem, positions, mem_norm_g, mem_w_kv, a_norm_g, a_w_in, a_w_out, b_norm_g, b_w_in, b_shift_mu, b_w0, b_w_up, b_a0, b_a_up, b_v0, b_v_up, b_g_up, b_k_k, b_k_a, b_r_k, b_ln_g, b_ln_b, b_w_out, ffn_norm_g, ffn_w_up, ffn_conv_w, ffn_conv_b, ffn_w_down, final_norm_g):
    batch, seq_len, d_model = x.shape
    t = batch * seq_len
    depth = ffn_w_up.shape[0]
    xw = d_model // 4
    mix_w = d_model - xw
    x = x.reshape(t, d_model)

    mem_h = rmsnorm(mem.reshape(-1, d_model), mem_norm_g, bf16)
    mem_kv = matmul(mem_h, mem_w_kv[None], 0)
    tabs = rope_tables(positions)

    def mixer_gain(layer):
        return (a_norm_g if layer % 2 == 0 else b_norm_g)[layer // 2]

    p_a = None
    h, row_ss = rmsnorm(x, mixer_gain(0), bf16), None
    for i in range(depth):
        j = i // 2
        if i % 2 == 0:
            p = matmul(h, a_w_in, j, row_ss=row_ss)
            if p_a is None:
                p_a = p
            mix = dilated_attn(p, tabs, batch, seq_len)
            xo = cross_attn(p, 3 * mix_w // xw, mem_kv, batch, seq_len)
            w_out = a_w_out
        else:
            low_w = b_w_in.shape[2] - 3 * mix_w - xw
            w_bt = jnp.swapaxes(b_w_in, 1, 2)
            p_rkv = matmul(h, w_bt, j, w_is_transposed=True, n=3 * mix_w, row_ss=row_ss)
            p_low = matmul(h, w_bt[j:j + 1, 3 * mix_w:3 * mix_w + low_w], 0, w_is_transposed=True,
                           row_ss=row_ss)
            cq = matmul(h, w_bt[j:j + 1, 3 * mix_w + low_w:], 0, w_is_transposed=True,
                        row_ss=row_ss, out_dtype=bf16)
            dd, da, dv = b_w_up.shape[1], b_a_up.shape[1], b_v_up.shape[1]
            prep = rwkv_prep(
                p_rkv, p_low, p_a, mix_w, seq_len, b_shift_mu[j], b_w0[j], b_a0[j], b_v0[j],
                b_k_k[j], b_k_a[j],
                _pad_rows(b_w_up[j], 0, low_w), _pad_rows(b_a_up[j], dd, low_w),
                _pad_rows(b_v_up[j], dd + da, low_w), _pad_rows(b_g_up[j], dd + da + dv, low_w))
            mix = rwkv_scan(*prep, b_ln_g[j], b_ln_b[j], b_r_k[j], batch, seq_len)
            xo = cross_attn(cq, 0, mem_kv, batch, seq_len)
            w_out = b_w_out
        x, normed = out_proj(mix, xo, w_out, j, x, ffn_norm_g[i])
        act = ffn_up(normed, ffn_w_up, ffn_conv_w, ffn_conv_b, i, seq_len)
        if i + 1 < depth:
            x, (h, row_ss) = matmul(act, ffn_w_down, i, res=x, norm_gain=mixer_gain(i + 1),
                                    tn=1024, tk=2048)
        else:
            x = matmul(act, ffn_w_down, i, res=x, tn=1024, tk=2048)
    return rmsnorm(x, final_norm_g, f32).reshape(batch, seq_len, d_model)
```

```python
import functools
import math

import jax
import jax.numpy as jnp
from jax import lax
from jax.experimental import pallas as pl
from jax.experimental.pallas import tpu as pltpu

f32 = jnp.float32
bf16 = jnp.bfloat16

X_HEADS = 4
A_HEAD_DIM = 128
A_GROUPS = ((128, 1), (512, 4), (2048, 16))
ATT_BLOCK = 128
ATT_UNITS_PER_BATCH = 8
ROT_DIM = A_HEAD_DIM // 4
ROPE_THETA = 500000.0
R_HEAD_DIM = 64
CONV_WIDTH = 3
NORM_EPS = 1e-6
GN_EPS = 64e-5
L2_EPS = 1e-12
NEG_INF = -1e30

V7X_VMEM_BYTES = 64 * 2**20
VMEM_LIMIT_BYTES = V7X_VMEM_BYTES - 4 * 2**20
SUBLANES = 8
LANES = 128
IN_PROJ_ROWS = 2048
MATMUL_TEMP_BYTES = 3 * 2**20

SCAN_CHUNK = LANES // 2
HEADS_PER_LANE_GROUP = LANES // R_HEAD_DIM


def _pick_tile(n, preferred):
    if n % preferred == 0:
        return preferred
    g = math.gcd(n, preferred)
    return g if g % LANES == 0 else n


def _params(semantics):
    return pltpu.CompilerParams(dimension_semantics=semantics,
                                vmem_limit_bytes=VMEM_LIMIT_BYTES)


def _dot(a, b):
    return jnp.dot(a.astype(bf16), b.astype(bf16), preferred_element_type=f32)


def _dot_nt(a, b):
    return lax.dot_general(a.astype(bf16), b.astype(bf16), (((1,), (1,)), ((), ())),
                           preferred_element_type=f32)


def _dot_tn(a, b):
    return lax.dot_general(a.astype(bf16), b.astype(bf16), (((0,), (0,)), ((), ())),
                           preferred_element_type=f32)


def _sigmoid(x):
    return 1.0 / (1.0 + jnp.exp(-x))


def _split_dot(x, ones_mat):
    hi = x.astype(bf16)
    lo = (x - hi.astype(f32)).astype(bf16)
    return (jnp.dot(hi, ones_mat, preferred_element_type=f32)
            + jnp.dot(lo, ones_mat, preferred_element_type=f32))


def _head_block_ones(width, head_dim):
    shift = int(math.log2(head_dim))
    r = lax.broadcasted_iota(jnp.int32, (width, width), 0) >> shift
    c = lax.broadcasted_iota(jnp.int32, (width, width), 1) >> shift
    return jnp.where(r == c, 1.0, 0.0).astype(bf16)


def _rmsnorm_kernel(x_ref, g_ref, o_ref):
    x = x_ref[...].astype(f32)
    ms = jnp.mean(x * x, axis=-1, keepdims=True)
    o_ref[...] = (x * lax.rsqrt(ms + NORM_EPS) * g_ref[...]).astype(o_ref.dtype)


def rmsnorm(x, g, out_dtype, tm=256):
    m, d = x.shape
    tm = min(tm, m)
    return pl.pallas_call(
        _rmsnorm_kernel,
        out_shape=jax.ShapeDtypeStruct((m, d), out_dtype),
        grid=(m // tm,),
        in_specs=[pl.BlockSpec((tm, d), lambda i: (i, 0)),
                  pl.BlockSpec((1, d), lambda i: (0, 0))],
        out_specs=pl.BlockSpec((tm, d), lambda i: (i, 0)),
        compiler_params=_params(("parallel",)),
        name="rmsnorm",
    )(x, g.reshape(1, d))


def _store_normed(x, g_ref, hn_ref, ss_ref):
    hn_ref[...] = (x * g_ref[...]).astype(hn_ref.dtype)
    ss_ref[...] = jnp.sum(x * x, axis=-1, keepdims=True)


def _row_rstd(ss_ref, width):
    return lax.rsqrt(jnp.sum(ss_ref[...], axis=0) * (1.0 / width) + NORM_EPS)


def _combine_ss_kernel(ss_ref, o_ref):
    o_ref[...] = jnp.sum(ss_ref[...], axis=0, keepdims=True)


def combine_row_ss(ss, tm=1024):
    nj, m, _ = ss.shape
    tm = _pick_tile(m, tm)
    return pl.pallas_call(
        _combine_ss_kernel,
        out_shape=jax.ShapeDtypeStruct((1, m, 1), f32),
        grid=(m // tm,),
        in_specs=[pl.BlockSpec((nj, tm, 1), lambda i: (0, i, 0))],
        out_specs=pl.BlockSpec((1, tm, 1), lambda i: (0, i, 0)),
        compiler_params=_params(("parallel",)),
        name="combine_row_ss",
    )(ss)


def _normed_outputs(m, n, tm, tn, index_map_2d, index_map_ss):
    shapes = [jax.ShapeDtypeStruct((m, n), bf16), jax.ShapeDtypeStruct((n // tn, m, 1), f32)]
    specs = [pl.BlockSpec((tm, tn), index_map_2d), pl.BlockSpec((None, tm, 1), index_map_ss)]
    return shapes, specs


def _matmul_kernel(*refs, nk, has_res, has_row_ss, emits_normed, w_is_transposed):
    a_ref, w_ref = refs[:2]
    rest = list(refs[2:])
    res_ref = rest.pop(0) if has_res else None
    row_ss_ref = rest.pop(0) if has_row_ss else None
    gain_ref = rest.pop(0) if emits_normed else None
    o_ref = rest.pop(0)
    hn_ref, ss_ref = (rest.pop(0), rest.pop(0)) if emits_normed else (None, None)
    dot = _dot_nt if w_is_transposed else _dot

    def finish(out):
        o_ref[...] = out.astype(o_ref.dtype)
        if emits_normed:
            _store_normed(out, gain_ref, hn_ref, ss_ref)

    if nk == 1:
        out = dot(a_ref[...], w_ref[...])
        if has_row_ss:
            out = out * _row_rstd(row_ss_ref, a_ref.shape[1])
        if has_res:
            out = out + res_ref[...]
        finish(out)
        return
    acc_ref, = rest
    k = pl.program_id(2)

    @pl.when(k == 0)
    def _():
        acc_ref[...] = res_ref[...] if has_res else jnp.zeros_like(acc_ref)

    acc_ref[...] += dot(a_ref[...], w_ref[...])

    @pl.when(k == nk - 1)
    def _():
        finish(acc_ref[...])


def matmul(a, w, layer, *, w_is_transposed=False, n=None, res=None, row_ss=None, norm_gain=None,
           out_dtype=f32, tm=1024, tn=512, tk=None):
    m, k_dim = a.shape
    if n is None:
        n = w.shape[1] if w_is_transposed else w.shape[2]
    tm, tn = _pick_tile(m, tm), _pick_tile(n, tn)
    tk = k_dim if tk is None else _pick_tile(k_dim, tk)
    nk = k_dim // tk

    def vmem_estimate(rows):
        blocks = rows * tk * a.dtype.itemsize + tk * tn * 4
        blocks += rows * tn * jnp.dtype(out_dtype).itemsize
        blocks += rows * tn * 4 * (res is not None)
        blocks += rows * LANES * 4 * (row_ss is not None)
        blocks += (rows * tn * 2 + rows * LANES * 4) * (norm_gain is not None)
        return 2 * blocks + rows * tn * 4 * (nk > 1)

    while vmem_estimate(tm) > VMEM_LIMIT_BYTES - MATMUL_TEMP_BYTES and tm % (2 * SUBLANES) == 0:
        tm //= 2
    assert m % tm == 0 and n % tn == 0 and k_dim % tk == 0, (a.shape, w.shape, tm, tn, tk)
    assert row_ss is None or nk == 1
    if w_is_transposed:
        w_spec = pl.BlockSpec((None, tn, tk), lambda i, j, k: (layer, j, k))
    else:
        w_spec = pl.BlockSpec((None, tk, tn), lambda i, j, k: (layer, k, j))
    in_specs = [pl.BlockSpec((tm, tk), lambda i, j, k: (i, k)), w_spec]
    args = [a, w]
    if res is not None:
        in_specs.append(pl.BlockSpec((tm, tn), lambda i, j, k: (i, j)))
        args.append(res)
    if row_ss is not None:
        in_specs.append(pl.BlockSpec((row_ss.shape[0], tm, 1), lambda i, j, k: (0, i, 0)))
        args.append(row_ss)
    out_shape = [jax.ShapeDtypeStruct((m, n), out_dtype)]
    out_specs = [pl.BlockSpec((tm, tn), lambda i, j, k: (i, j))]
    if norm_gain is not None:
        in_specs.append(pl.BlockSpec((1, tn), lambda i, j, k: (0, j)))
        args.append(norm_gain.reshape(1, n))
        shapes, specs = _normed_outputs(m, n, tm, tn, lambda i, j, k: (i, j), lambda i, j, k: (j, i, 0))
        out_shape += shapes
        out_specs += specs
    outs = pl.pallas_call(
        functools.partial(_matmul_kernel, nk=nk, has_res=res is not None,
                          has_row_ss=row_ss is not None, emits_normed=norm_gain is not None,
                          w_is_transposed=w_is_transposed),
        out_shape=out_shape,
        grid=(m // tm, n // tn, nk),
        in_specs=in_specs,
        out_specs=out_specs,
        scratch_shapes=[pltpu.VMEM((tm, tn), f32)] if nk > 1 else [],
        compiler_params=_params(("parallel", "parallel", "arbitrary")),
        name="matmul",
    )(*args)
    return outs[0] if norm_gain is None else (outs[0], (outs[1], combine_row_ss(outs[2])))


def _out_proj_kernel(mix_ref, xo_ref, w_ref, res_ref, gain_ref, o_ref, hn_ref, ss_ref):
    k_mix = mix_ref.shape[1]
    out = (res_ref[...] + _dot(mix_ref[...], w_ref[:k_mix, :])
           + _dot(xo_ref[...], w_ref[k_mix:, :]))
    o_ref[...] = out
    _store_normed(out, gain_ref, hn_ref, ss_ref)


def _out_proj_tiles(m, k_total, n):
    for tm, tn in ((2048, 512), (2048, 256), (1024, 512), (1024, 256), (512, 256)):
        tm, tn = _pick_tile(m, tm), _pick_tile(n, tn)
        acts = 2 * tm * k_total * 2
        weights = 2 * k_total * tn * 4 + k_total * tn * 2
        tiles = 2 * tm * tn * (4 + 4 + 2) + 2 * tm * LANES * 4
        temps = 2 * tm * tn * 4
        if acts + weights + tiles + temps <= VMEM_LIMIT_BYTES - MATMUL_TEMP_BYTES:
            break
    return tm, tn


def out_proj(mix, xo, w, layer, res, norm_gain):
    m, k_mix = mix.shape
    k_xo = xo.shape[1]
    n = w.shape[2]
    tm, tn = _out_proj_tiles(m, k_mix + k_xo, n)
    assert w.shape[1] == k_mix + k_xo
    shapes, specs = _normed_outputs(m, n, tm, tn, lambda i, j: (i, j), lambda i, j: (j, i, 0))
    x, hn, ss = pl.pallas_call(
        _out_proj_kernel,
        out_shape=[jax.ShapeDtypeStruct((m, n), f32)] + shapes,
        grid=(m // tm, n // tn),
        in_specs=[pl.BlockSpec((tm, k_mix), lambda i, j: (i, 0)),
                  pl.BlockSpec((tm, k_xo), lambda i, j: (i, 0)),
                  pl.BlockSpec((None, k_mix + k_xo, tn), lambda i, j: (layer, 0, j)),
                  pl.BlockSpec((tm, tn), lambda i, j: (i, j)),
                  pl.BlockSpec((1, tn), lambda i, j: (0, j))],
        out_specs=[pl.BlockSpec((tm, tn), lambda i, j: (i, j))] + specs,
        compiler_params=_params(("parallel", "parallel")),
        name="out_proj",
    )(mix, xo, w, res, norm_gain.reshape(1, n))
    return x, (hn, combine_row_ss(ss))


def _ffn_up_kernel(a_ref, ss_ref, wg_ref, wu_ref, cw_ref, cb_ref, o_ref, wg_bf, wu_bf, tail_ref,
                   *, tiles_per_seq):
    i = pl.program_id(1)

    @pl.when(i == 0)
    def _():
        wg_bf[...] = wg_ref[...].astype(bf16)
        wu_bf[...] = wu_ref[...].astype(bf16)

    a = a_ref[...]
    rstd = _row_rstd(ss_ref, a.shape[1])
    gate = jnp.dot(a, wg_bf[...], preferred_element_type=f32) * rstd
    up = jnp.dot(a, wu_bf[...], preferred_element_type=f32) * rstd
    tm = gate.shape[0]

    tail = jnp.where(i % tiles_per_seq == 0, 0.0, tail_ref[...])
    tail_ref[...] = gate[tm - SUBLANES:, :]

    row = lax.broadcasted_iota(jnp.int32, gate.shape, 0)
    last, last2 = tail[SUBLANES - 1:SUBLANES, :], tail[SUBLANES - 2:SUBLANES - 1, :]
    g1 = jnp.where(row == 0, last, pltpu.roll(gate, 1, 0))
    g2 = jnp.where(row == 0, last2, jnp.where(row == 1, last, pltpu.roll(gate, 2, 0)))
    cw = cw_ref[...]
    conv = cb_ref[...] + cw[0:1, :] * g2 + cw[1:2, :] * g1 + cw[2:3, :] * gate
    o_ref[...] = (conv * _sigmoid(conv) * up).astype(o_ref.dtype)


def ffn_up(normed, w_up, conv_w, conv_b, layer, seq_len, *, tm=512, tn=512):
    h, row_ss = normed
    m, d = h.shape
    d_ff = w_up.shape[2] // 2
    tm, tn = _pick_tile(seq_len, tm), _pick_tile(d_ff, tn)
    assert m % tm == 0 and seq_len % tm == 0 and d_ff % tn == 0
    nj = d_ff // tn
    depth = conv_w.shape[0]
    return pl.pallas_call(
        functools.partial(_ffn_up_kernel, tiles_per_seq=seq_len // tm),
        out_shape=jax.ShapeDtypeStruct((m, d_ff), bf16),
        grid=(nj, m // tm),
        in_specs=[pl.BlockSpec((tm, d), lambda j, i: (i, 0)),
                  pl.BlockSpec((row_ss.shape[0], tm, 1), lambda j, i: (0, i, 0)),
                  pl.BlockSpec((None, d, tn), lambda j, i: (layer, 0, j)),
                  pl.BlockSpec((None, d, tn), lambda j, i: (layer, 0, j + nj)),
                  pl.BlockSpec((None, CONV_WIDTH, tn), lambda j, i: (layer, 0, j)),
                  pl.BlockSpec((None, 1, tn), lambda j, i: (layer, 0, j))],
        out_specs=pl.BlockSpec((tm, tn), lambda j, i: (i, j)),
        scratch_shapes=[pltpu.VMEM((d, tn), bf16), pltpu.VMEM((d, tn), bf16),
                        pltpu.VMEM((SUBLANES, tn), f32)],
        compiler_params=_params(("arbitrary", "arbitrary")),
        name="ffn_up",
    )(h, row_ss, w_up, w_up, conv_w, conv_b.reshape(depth, 1, d_ff))


def _rope_table_kernel(pos_ref, freq_ref, o_ref):
    ang = pos_ref[...].astype(f32) * freq_ref[...]
    lane = lax.broadcasted_iota(jnp.int32, ang.shape, 1)
    o_ref[0] = jnp.where(lane < ROT_DIM, jnp.cos(ang), 1.0)
    o_ref[1] = jnp.where(lane < ROT_DIM, jnp.sin(ang), 0.0)


def rope_tables(positions, tm=512):
    t = positions.size
    half = ROT_DIM // 2
    inv_freq = ROPE_THETA ** (-jnp.arange(half, dtype=f32) / half)
    freq = jnp.zeros((1, A_HEAD_DIM), f32).at[0, :ROT_DIM].set(jnp.tile(inv_freq, 2))
    tm = _pick_tile(t, tm)
    return pl.pallas_call(
        _rope_table_kernel,
        out_shape=jax.ShapeDtypeStruct((2, t, LANES), f32),
        grid=(t // tm,),
        in_specs=[pl.BlockSpec((tm, 1), lambda i: (i, 0)),
                  pl.BlockSpec((1, A_HEAD_DIM), lambda i: (0, 0))],
        out_specs=pl.BlockSpec((2, tm, LANES), lambda i: (0, i, 0)),
        compiler_params=_params(("parallel",)),
        name="rope_tables",
    )(positions.reshape(t, 1), freq)


def _rotate_half_matrix():
    half = ROT_DIM // 2
    src = lax.broadcasted_iota(jnp.int32, (A_HEAD_DIM, A_HEAD_DIM), 0)
    dst = lax.broadcasted_iota(jnp.int32, (A_HEAD_DIM, A_HEAD_DIM), 1)
    minus = (dst < half) & (src == dst + half)
    plus = (dst >= half) & (dst < ROT_DIM) & (src == dst - half)
    return jnp.where(minus, -1.0, jnp.where(plus, 1.0, 0.0)).astype(bf16)


def _rope(x, cos, sin, rot_mat):
    return x * cos + _split_dot(x, rot_mat) * sin


def _dilated_attn_kernel(*refs, dilations, n_backs):
    ng = len(dilations)
    q_refs, kc_refs, kp_refs = refs[0:ng], refs[ng:2 * ng], refs[2 * ng:3 * ng]
    vc_refs, vp_refs = refs[3 * ng:4 * ng], refs[4 * ng:5 * ng]
    tc_ref, tp_ref, o_ref = refs[5 * ng:5 * ng + 3]
    scratch = refs[5 * ng + 3:]
    o_scr, lse_scr, q_scr, k_scr, kp_scr = (scratch[i * ng:(i + 1) * ng] for i in range(5))

    blk = ATT_BLOCK
    span = q_refs[0].shape[0]
    first_span = pl.program_id(1) == 0
    qi = lax.broadcasted_iota(jnp.int32, (blk, 2 * blk), 0)
    col = lax.broadcasted_iota(jnp.int32, (blk, 2 * blk), 1)
    dist = qi + blk - col
    first_col = jnp.where(first_span, blk, 0)
    scale = A_HEAD_DIM ** -0.5

    cos_c, sin_c = tc_ref[0], tc_ref[1]
    rot_mat = _rotate_half_matrix()
    for g, d in enumerate(dilations):
        tail = pl.ds(span - d * blk, d * blk)
        q_scr[g][...] = _rope(q_refs[g][...], cos_c, sin_c, rot_mat) * scale
        k_scr[g][...] = _rope(kc_refs[g][...], cos_c, sin_c, rot_mat)
        kp_scr[g][...] = _rope(kp_refs[g][...], tp_ref[0, tail, :], tp_ref[1, tail, :], rot_mat)

    ones_kv = jnp.ones((2 * blk, A_HEAD_DIM), bf16)

    def rows(start, size, d):
        return pl.ds(start, size, stride=d) if d > 1 else pl.ds(start, size)

    def load_unit(g, d, r, j):
        span_g = d * blk
        cur = rows(j * span_g + r, blk, d)
        if j == 0:
            prev = rows(r, blk, d)
            k2 = jnp.concatenate([kp_scr[g][prev, :], k_scr[g][cur, :]], axis=0)
            v2 = jnp.concatenate([vp_refs[g][prev, :], vc_refs[g][cur, :]], axis=0)
        else:
            both = rows((j - 1) * span_g + r, 2 * blk, d)
            k2 = k_scr[g][both, :]
            v2 = vc_refs[g][both, :]
        return q_scr[g][cur, :].astype(bf16), k2.astype(bf16), v2.astype(bf16)

    for g, (d, n_back) in enumerate(zip(dilations, n_backs)):
        in_window = (dist >= 0) & (dist <= n_back)
        in_window_first = in_window & (col >= first_col)
        units = [(r, j) for r in range(d) for j in range(span // (d * blk))]
        for b0 in range(0, len(units), ATT_UNITS_PER_BATCH):
            batch_units = units[b0:b0 + ATT_UNITS_PER_BATCH]
            loaded = [load_unit(g, d, r, j) for r, j in batch_units]
            scores = [jnp.where(in_window_first if j == 0 else in_window, _dot_nt(q, k2), NEG_INF)
                      for (q, k2, _), (_, j) in zip(loaded, batch_units)]
            maxes = [jnp.max(s, axis=-1, keepdims=True) for s in scores]
            probs = [jnp.exp(s - m).astype(bf16) for s, m in zip(scores, maxes)]
            sums = [jnp.dot(p, ones_kv, preferred_element_type=f32) for p in probs]
            outs = [_dot(p, v2) * (1.0 / l) for p, (_, _, v2), l in zip(probs, loaded, sums)]
            for (r, j), o, m, l in zip(batch_units, outs, maxes, sums):
                cur = rows(j * d * blk + r, blk, d)
                o_scr[g][cur, :] = o
                lse_scr[g][cur, :] = m + jnp.log(l)

    lses = [ref[...] for ref in lse_scr]
    m = functools.reduce(jnp.maximum, lses)
    wts = [jnp.exp(x - m) for x in lses]
    num = sum(w * ref[...] for w, ref in zip(wts, o_scr))
    o_ref[...] = (num / sum(wts)).astype(o_ref.dtype)


def dilated_attn(p_in, tabs, batch, seq_len):
    t, ncol = p_in.shape
    ng = len(A_GROUPS)
    gw = ncol // (3 * ng + 1)
    heads = gw // A_HEAD_DIM
    dilations = tuple(d for _, d in A_GROUPS)
    n_backs = tuple(w // d for w, d in A_GROUPS)
    span = max(dilations) * ATT_BLOCK
    assert all(nb <= ATT_BLOCK for nb in n_backs) and seq_len % span == 0
    assert all(span % (d * ATT_BLOCK) == 0 for d in dilations)
    nspan = seq_len // span

    def cur(part, g):
        return pl.BlockSpec((span, A_HEAD_DIM),
                            lambda b, n, h: (b * nspan + n, (part * ng + g) * heads + h))

    def prev(part, g):
        per_span = span // (dilations[g] * ATT_BLOCK)
        return pl.BlockSpec(
            (dilations[g] * ATT_BLOCK, A_HEAD_DIM),
            lambda b, n, h: (jnp.maximum((b * nspan + n) * per_span - 1, 0), (part * ng + g) * heads + h))

    groups = range(ng)
    in_specs = ([cur(0, g) for g in groups] + [cur(1, g) for g in groups] + [prev(1, g) for g in groups]
                + [cur(2, g) for g in groups] + [prev(2, g) for g in groups]
                + [pl.BlockSpec((2, span, LANES), lambda b, n, h: (0, b * nspan + n, 0)),
                   pl.BlockSpec((2, span, LANES), lambda b, n, h: (0, jnp.maximum(b * nspan + n - 1, 0), 0))])
    return pl.pallas_call(
        functools.partial(_dilated_attn_kernel, dilations=dilations, n_backs=n_backs),
        out_shape=jax.ShapeDtypeStruct((t, gw), bf16),
        grid=(batch, nspan, heads),
        in_specs=in_specs,
        out_specs=pl.BlockSpec((span, A_HEAD_DIM), lambda b, n, h: (b * nspan + n, h)),
        scratch_shapes=([pltpu.VMEM((span, A_HEAD_DIM), f32)] * (4 * ng)
                        + [pltpu.VMEM((d * ATT_BLOCK, A_HEAD_DIM), f32) for d in dilations]),
        compiler_params=_params(("parallel", "parallel", "arbitrary")),
        name="dilated_attn",
    )(*([p_in] * (5 * ng)), tabs, tabs)


def _xattn_kernel(q_ref, k_ref, v_ref, o_ref, *, heads):
    hd = q_ref.shape[1] // heads
    scale = hd ** -0.5
    for h in range(heads):
        sl = slice(h * hd, (h + 1) * hd)
        s = _dot_nt(q_ref[:, sl], k_ref[:, sl]) * scale
        m = jnp.max(s, axis=-1, keepdims=True)
        p = jnp.exp(s - m)
        l = jnp.sum(p, axis=-1, keepdims=True)
        o_ref[:, sl] = _dot(p / l, v_ref[:, sl]).astype(o_ref.dtype)


def cross_attn(q_arr, q_col_block, mem_kv, batch, seq_len, tq=512):
    xw = mem_kv.shape[1] // 2
    mem_len = mem_kv.shape[0] // batch
    tq = _pick_tile(seq_len, tq)
    nq = seq_len // tq
    return pl.pallas_call(
        functools.partial(_xattn_kernel, heads=X_HEADS),
        out_shape=jax.ShapeDtypeStruct((batch * seq_len, xw), bf16),
        grid=(batch, nq),
        in_specs=[pl.BlockSpec((tq, xw), lambda b, i: (b * nq + i, q_col_block)),
                  pl.BlockSpec((mem_len, xw), lambda b, i: (b, 0)),
                  pl.BlockSpec((mem_len, xw), lambda b, i: (b, 1))],
        out_specs=pl.BlockSpec((tq, xw), lambda b, i: (b * nq + i, 0)),
        compiler_params=_params(("parallel", "parallel")),
        name="cross_attn",
    )(q_arr, mem_kv, mem_kv)


def _shift(x, last_rows, mu, seq_start):
    row = lax.broadcasted_iota(jnp.int32, x.shape, 0)
    last = jnp.where(seq_start, 0.0, last_rows[SUBLANES - 1:SUBLANES, :])
    prev = jnp.where(row == 0, last, pltpu.roll(x, 1, 0))
    return x + (prev - x) * mu


def _low_rank_acts_kernel(lo_ref, halo_ref, mu_ref, o_ref, *, tiles_per_seq, segments):
    seq_start = pl.program_id(0) % tiles_per_seq == 0
    low = _shift(lo_ref[...], halo_ref[...], mu_ref[...], seq_start)
    (w0_, w1_), (a0_, a1_), (v0_, v1_), (g0_, g1_) = segments
    o_ref[:, w0_:w1_] = jnp.tanh(low[:, w0_:w1_]).astype(o_ref.dtype)
    o_ref[:, a0_:v1_] = low[:, a0_:v1_].astype(o_ref.dtype)
    o_ref[:, g0_:g1_] = _sigmoid(low[:, g0_:g1_]).astype(o_ref.dtype)


def low_rank_acts(p_low, mu_low, seq_len, segments, tm=512):
    t, width = p_low.shape
    tm = _pick_tile(seq_len, tm)
    hb = tm // SUBLANES
    return pl.pallas_call(
        functools.partial(_low_rank_acts_kernel, tiles_per_seq=seq_len // tm, segments=segments),
        out_shape=jax.ShapeDtypeStruct((t, width), bf16),
        grid=(t // tm,),
        in_specs=[pl.BlockSpec((tm, width), lambda i: (i, 0)),
                  pl.BlockSpec((SUBLANES, width), lambda i: (jnp.maximum(i * hb - 1, 0), 0)),
                  pl.BlockSpec((1, width), lambda i: (0, 0))],
        out_specs=pl.BlockSpec((tm, width), lambda i: (i, 0)),
        compiler_params=_params(("parallel",)),
        name="low_rank_acts",
    )(p_low, p_low, mu_low)


def _rwkv_project_kernel(hn_ref, ss_ref, wr_ref, wk_ref, wv_ref, la_ref, vf_ref,
                         mu_r, mu_k, mu_v, w0, a0, v0, k_k, k_a, wup, aup, vup, gup,
                         r_o, lw_o, k_o, v_o, kk_o, b_o, g_o,
                         wr_bf, wk_bf, wv_bf, r_tail, k_tail, v_tail, *, tiles_per_seq, segments):
    i = pl.program_id(1)

    @pl.when(i == 0)
    def _():
        wr_bf[...] = wr_ref[...].astype(bf16)
        wk_bf[...] = wk_ref[...].astype(bf16)
        wv_bf[...] = wv_ref[...].astype(bf16)

    seq_start = i % tiles_per_seq == 0
    a = hn_ref[...]
    rstd = _row_rstd(ss_ref, a.shape[1])
    tm = a.shape[0]

    (w0_, w1_), (a0_, a1_), (v0_, v1_), (g0_, g1_) = segments
    la = la_ref[...]
    xw = w0[...] + _dot(la[:, w0_:w1_], wup[...])
    lw_o[...] = -math.exp(-0.5) * _sigmoid(xw)
    ag = _sigmoid(a0[...] + _dot(la[:, a0_:a1_], aup[...]))
    v_gate = _sigmoid(v0[...] + _dot(la[:, v0_:v1_], vup[...]))
    g_o[...] = _dot(la[:, g0_:g1_], gup[...])

    def project(w_bf, tail_ref, mu_ref):
        raw = _dot_nt(a, w_bf[...]) * rstd
        shifted = _shift(raw, tail_ref[...], mu_ref[...], seq_start)
        tail_ref[...] = raw[tm - SUBLANES:, :]
        return shifted

    r_o[...] = project(wr_bf, r_tail, mu_r)

    k = project(wk_bf, k_tail, mu_k)
    kk = k * k_k[...]
    ones = _head_block_ones(kk.shape[1], R_HEAD_DIM)
    norm = jnp.sqrt(_split_dot(kk * kk, ones))
    kk = kk / jnp.maximum(norm, L2_EPS)
    kk_o[...] = kk
    b_o[...] = kk * ag
    k_o[...] = k * (1.0 + (ag - 1.0) * k_a[...])

    v = project(wv_bf, v_tail, mu_v)
    v_o[...] = v + (vf_ref[...] - v) * v_gate


def rwkv_project(normed, w_t, layer, low_acts, segments, p_a, mix_w, seq_len, mu, w0, a0, v0,
                 k_k, k_a, wup, aup, vup, gup, *, tm=512, tn=256):
    hn, row_ss = normed
    t, d = hn.shape
    tm, tn = _pick_tile(seq_len, tm), _pick_tile(mix_w, tn)
    assert seq_len % tm == 0 and mix_w % tn == 0 and tn % LANES == 0
    nc = mix_w // tn
    low_width = low_acts.shape[1]

    def w_rows(part):
        return pl.BlockSpec((None, tn, d), lambda j, i: (layer, part * nc + j, 0))

    def row(part=0):
        return pl.BlockSpec((1, tn), lambda j, i: (0, part * nc + j))

    def lora(ref):
        return pl.BlockSpec((ref.shape[0], tn), lambda j, i: (0, j))

    out_spec = pl.BlockSpec((tm, tn), lambda j, i: (i, j))
    mu_full = mu.reshape(1, -1)
    in_specs = [pl.BlockSpec((tm, d), lambda j, i: (i, 0)),
                pl.BlockSpec((row_ss.shape[0], tm, 1), lambda j, i: (0, i, 0)),
                w_rows(0), w_rows(1), w_rows(2),
                pl.BlockSpec((tm, low_width), lambda j, i: (i, 0)),
                pl.BlockSpec((tm, tn), lambda j, i: (i, 2 * nc + j)),
                row(0), row(1), row(2), row(), row(), row(), row(), row(),
                lora(wup), lora(aup), lora(vup), lora(gup)]
    return pl.pallas_call(
        functools.partial(_rwkv_project_kernel, tiles_per_seq=seq_len // tm, segments=segments),
        out_shape=(jax.ShapeDtypeStruct((t, mix_w), f32),) * 7,
        grid=(nc, t // tm),
        in_specs=in_specs,
        out_specs=(out_spec,) * 7,
        scratch_shapes=[pltpu.VMEM((tn, d), bf16)] * 3 + [pltpu.VMEM((SUBLANES, tn), f32)] * 3,
        compiler_params=_params(("arbitrary", "arbitrary")),
        name="rwkv_project",
    )(hn, row_ss, w_t, w_t, w_t, low_acts, p_a, mu_full, mu_full, mu_full,
      w0.reshape(1, -1), a0.reshape(1, -1), v0.reshape(1, -1), k_k.reshape(1, -1),
      k_a.reshape(1, -1), wup, aup, vup, gup)


def _rwkv_scan_kernel(r_ref, lw_ref, k_ref, v_ref, kk_ref, b_ref, g_ref, lng_ref, lnb_ref, rk_ref,
                      o_ref, state_ref):
    @pl.when(pl.program_id(2) == 0)
    def _():
        state_ref[...] = jnp.zeros_like(state_ref)

    c = SCAN_CHUNK
    n = 2 * c
    pairs = range(r_ref.shape[1] // LANES)
    cols = [slice(p * LANES, (p + 1) * LANES) for p in pairs]

    tri_r = lax.broadcasted_iota(jnp.int32, (c, c), 0)
    tri_c = lax.broadcasted_iota(jnp.int32, (c, c), 1)
    tri = jnp.where(tri_r >= tri_c, 1.0, 0.0)
    lw_all = lw_ref[...]
    lcum_all = jnp.dot(tri, lw_all, preferred_element_type=f32, precision=lax.Precision.HIGHEST)

    head0 = lax.broadcasted_iota(jnp.int32, (c, LANES), 1) < R_HEAD_DIM
    row = lax.broadcasted_iota(jnp.int32, (n, n), 0)
    col = lax.broadcasted_iota(jnp.int32, (n, n), 1)
    rin, cin = row & (c - 1), col & (c - 1)
    strict, incl = rin > cin, rin >= cin
    eye = jnp.where(row == col, 1.0, 0.0)

    def merge_mask(level):
        rb, cb = row >> level, col >> level
        return ((rb >> 1) == (cb >> 1)) & ((rb & 1) == 1) & ((cb & 1) == 0)

    def stack(x):
        return jnp.concatenate([jnp.where(head0, x, 0.0), jnp.where(head0, 0.0, x)], axis=0)

    ops = []
    for sl in cols:
        lcum, lw = lcum_all[:, sl], lw_all[:, sl]
        r, k, v, kk, b = r_ref[:, sl], k_ref[:, sl], v_ref[:, sl], kk_ref[:, sl], b_ref[:, sl]
        l_end = lcum[c - 1:c, :]
        e_in = jnp.exp(lcum)
        e_out = jnp.exp(-lcum)
        e_end = jnp.exp(l_end - lcum)
        ops.append(dict(
            a_t=stack(-kk * jnp.exp(lcum - lw)), r_t=stack(r * e_in),
            b_t=stack(b * e_out), k_t=stack(k * e_out),
            bk_h=jnp.concatenate([stack(b * e_end), stack(k * e_end)], axis=0),
            v_s=stack(v), decay=jnp.exp(l_end)))

    for o in ops:
        s1 = _dot_nt(jnp.concatenate([o["a_t"], o["r_t"]], axis=0),
                     jnp.concatenate([o["b_t"], o["k_t"]], axis=0))
        o["ab"] = jnp.where(strict, s1[:n, :n], 0.0)
        o["ak"] = jnp.where(strict, s1[:n, n:], 0.0)
        o["rbk"] = jnp.concatenate([jnp.where(incl, s1[n:, :n], 0.0),
                                    jnp.where(incl, s1[n:, n:], 0.0)], axis=1)

    mask = merge_mask(0)
    ts = [eye + jnp.where(mask, o["ab"], 0.0) for o in ops]
    for level in range(1, int(math.log2(c))):
        mask = merge_mask(level)
        zs = [_dot(jnp.where(mask, o["ab"], 0.0), t) for o, t in zip(ops, ts)]
        ts = [t + _dot(t, z) for t, z in zip(ts, zs)]

    states = [state_ref[p] for p in pairs]
    rhs = [_dot_nt(o["a_t"], s) + _dot(o["ak"], o["v_s"]) for o, s in zip(ops, states)]
    us = [_dot(t, x) for t, x in zip(ts, rhs)]
    uvs = [jnp.concatenate([u, o["v_s"]], axis=0) for u, o in zip(us, ops)]
    y_stacked = [_dot_nt(o["r_t"], s) + _dot(o["rbk"], uv) for o, s, uv in zip(ops, states, uvs)]
    for p, o, s, uv in zip(pairs, ops, states, uvs):
        state_ref[p] = s * o["decay"] + _dot_tn(uv, o["bk_h"])

    ones = _head_block_ones(LANES, R_HEAD_DIM)
    inv_n = 1.0 / R_HEAD_DIM
    ys = [y[:c, :] + y[c:, :] for y in y_stacked]
    mus = [_split_dot(y, ones) * inv_n for y in ys]
    ycs = [y - mu for y, mu in zip(ys, mus)]
    vars_ = [_split_dot(yc * yc, ones) * inv_n for yc in ycs]
    bonus = [_split_dot(r_ref[:, sl] * k_ref[:, sl] * rk_ref[:, sl], ones) for sl in cols]
    for sl, yc, var, bo in zip(cols, ycs, vars_, bonus):
        y = yc * lax.rsqrt(var + GN_EPS) * lng_ref[:, sl] + lnb_ref[:, sl]
        o_ref[:, sl] = ((y + bo * v_ref[:, sl]) * g_ref[:, sl]).astype(o_ref.dtype)


def rwkv_scan(r, lw, k, v, kk, b, g, ln_g, ln_b, r_k, batch, seq_len, *, wb=3072):
    t, mix_w = r.shape
    wb = _pick_tile(mix_w, wb)
    assert mix_w % wb == 0 and wb % LANES == 0 and seq_len % SCAN_CHUNK == 0
    nchunk = seq_len // SCAN_CHUNK
    tile = pl.BlockSpec((SCAN_CHUNK, wb), lambda bi, w, n: (bi * nchunk + n, w))
    row = pl.BlockSpec((1, wb), lambda bi, w, n: (0, w))
    return pl.pallas_call(
        _rwkv_scan_kernel,
        out_shape=jax.ShapeDtypeStruct((t, mix_w), bf16),
        grid=(batch, mix_w // wb, nchunk),
        in_specs=[tile] * 7 + [row] * 3,
        out_specs=tile,
        scratch_shapes=[pltpu.VMEM((wb // LANES, LANES, LANES), f32)],
        compiler_params=_params(("parallel", "parallel", "arbitrary")),
        name="rwkv_scan",
    )(r, lw, k, v, kk, b, g, ln_g.reshape(1, -1), ln_b.reshape(1, -1), r_k.reshape(1, -1))


def _round_up(n, multiple):
    return -(-n // multiple) * multiple


def _pack_segments(x, sizes, axis):
    parts, start = [], 0
    for s in sizes:
        seg = lax.slice_in_dim(x, start, start + s, axis=axis)
        pad = [(0, 0)] * x.ndim
        pad[axis] = (0, _round_up(s, LANES) - s)
        parts.append(jnp.pad(seg, pad))
        start += s
    return jnp.concatenate(parts, axis=axis)


def _pad_rows(w, start, total):
    return jnp.zeros((total, w.shape[1]), bf16).at[start:start + w.shape[0]].set(w.astype(bf16))


def kernel(x, mem, positions, mem_norm_g, mem_w_kv, a_norm_g, a_w_in, a_w_out, b_norm_g, b_w_in, b_shift_mu, b_w0, b_w_up, b_a0, b_a_up, b_v0, b_v_up, b_g_up, b_k_k, b_k_a, b_r_k, b_ln_g, b_ln_b, b_w_out, ffn_norm_g, ffn_w_up, ffn_conv_w, ffn_conv_b, ffn_w_down, final_norm_g):
    batch, seq_len, d_model = x.shape
    t = batch * seq_len
    depth = ffn_w_up.shape[0]
    xw = d_model // 4
    mix_w = d_model - xw
    x = x.reshape(t, d_model)

    mem_h = rmsnorm(mem.reshape(-1, d_model), mem_norm_g, bf16)
    mem_kv = matmul(mem_h, mem_w_kv[None], 0)
    tabs = rope_tables(positions)

    def mixer_gain(layer):
        return (a_norm_g if layer % 2 == 0 else b_norm_g)[layer // 2]

    p_a = None
    h, row_ss = rmsnorm(x, mixer_gain(0), bf16), None
    for i in range(depth):
        j = i // 2
        if i % 2 == 0:
            p = matmul(h, a_w_in, j, row_ss=row_ss, tm=IN_PROJ_ROWS)
            if p_a is None:
                p_a = p
            mix = dilated_attn(p, tabs, batch, seq_len)
            xo = cross_attn(p, 3 * mix_w // xw, mem_kv, batch, seq_len)
            w_out = a_w_out
        else:
            w_bt = jnp.swapaxes(b_w_in, 1, 2)
            ups = (b_w_up[j], b_a_up[j], b_v_up[j], b_g_up[j])
            sizes = [u.shape[0] for u in ups]
            low_w = sum(sizes)
            w_low_t = _pack_segments(w_bt[j, 3 * mix_w:3 * mix_w + low_w], sizes, 0)[None]
            mu_low = _pack_segments(b_shift_mu[j][None, 3 * mix_w:], sizes, 1)
            segments, start = [], 0
            for s in sizes:
                segments.append((start, start + _round_up(s, LANES)))
                start = segments[-1][1]
            p_low = matmul(h, w_low_t, 0, w_is_transposed=True, row_ss=row_ss, tn=start)
            low_acts = low_rank_acts(p_low, mu_low, seq_len, tuple(segments))
            cq = matmul(h, w_bt[j:j + 1, 3 * mix_w + low_w:], 0, w_is_transposed=True,
                        row_ss=row_ss, out_dtype=bf16)
            prep = rwkv_project(
                (h, row_ss), w_bt, j, low_acts, tuple(segments), p_a, mix_w, seq_len,
                b_shift_mu[j], b_w0[j], b_a0[j], b_v0[j], b_k_k[j], b_k_a[j],
                *[_pad_rows(u, 0, _round_up(u.shape[0], LANES)) for u in ups])
            mix = rwkv_scan(*prep, b_ln_g[j], b_ln_b[j], b_r_k[j], batch, seq_len)
            xo = cross_attn(cq, 0, mem_kv, batch, seq_len)
            w_out = b_w_out
        x, normed = out_proj(mix, xo, w_out, j, x, ffn_norm_g[i])
        act = ffn_up(normed, ffn_w_up, ffn_conv_w, ffn_conv_b, i, seq_len)
        if i + 1 < depth:
            x, (h, row_ss) = matmul(act, ffn_w_down, i, res=x, norm_gain=mixer_gain(i + 1),
                                    tn=1024, tk=2048)
        else:
            x = matmul(act, ffn_w_down, i, res=x, tn=1024, tk=2048)
    return rmsnorm(x, final_norm_g, f32).reshape(batch, seq_len, d_model)
```

```python
import functools
import math

import jax
import jax.numpy as jnp
from jax import lax
from jax.experimental import pallas as pl
from jax.experimental.pallas import tpu as pltpu

f32 = jnp.float32
bf16 = jnp.bfloat16

X_HEADS = 4
A_HEAD_DIM = 128
A_GROUPS = ((128, 1), (512, 4), (2048, 16))
ATT_BLOCK = 128
ATT_UNITS_PER_BATCH = 8
ROT_DIM = A_HEAD_DIM // 4
ROPE_THETA = 500000.0
R_HEAD_DIM = 64
CONV_WIDTH = 3
NORM_EPS = 1e-6
GN_EPS = 64e-5
L2_EPS = 1e-12
NEG_INF = -1e30

V7X_VMEM_BYTES = 64 * 2**20
VMEM_LIMIT_BYTES = V7X_VMEM_BYTES - 4 * 2**20
SUBLANES = 8
LANES = 128
IN_PROJ_ROWS = 2048
MATMUL_TEMP_BYTES = 3 * 2**20

SCAN_CHUNK = LANES // 2
HEADS_PER_LANE_GROUP = LANES // R_HEAD_DIM


def _pick_tile(n, preferred):
    if n % preferred == 0:
        return preferred
    g = math.gcd(n, preferred)
    return g if g % LANES == 0 else n


def _params(semantics):
    return pltpu.CompilerParams(dimension_semantics=semantics,
                                vmem_limit_bytes=VMEM_LIMIT_BYTES)


def _dot(a, b):
    return jnp.dot(a.astype(bf16), b.astype(bf16), preferred_element_type=f32)


def _dot_nt(a, b):
    return lax.dot_general(a.astype(bf16), b.astype(bf16), (((1,), (1,)), ((), ())),
                           preferred_element_type=f32)


def _dot_tn(a, b):
    return lax.dot_general(a.astype(bf16), b.astype(bf16), (((0,), (0,)), ((), ())),
                           preferred_element_type=f32)


def _sigmoid(x):
    return 1.0 / (1.0 + jnp.exp(-x))


def _split_dot(x, ones_mat):
    hi = x.astype(bf16)
    lo = (x - hi.astype(f32)).astype(bf16)
    return (jnp.dot(hi, ones_mat, preferred_element_type=f32)
            + jnp.dot(lo, ones_mat, preferred_element_type=f32))


def _head_block_ones(width, head_dim):
    shift = int(math.log2(head_dim))
    r = lax.broadcasted_iota(jnp.int32, (width, width), 0) >> shift
    c = lax.broadcasted_iota(jnp.int32, (width, width), 1) >> shift
    return jnp.where(r == c, 1.0, 0.0).astype(bf16)


def _rmsnorm_kernel(x_ref, g_ref, o_ref):
    x = x_ref[...].astype(f32)
    ms = jnp.mean(x * x, axis=-1, keepdims=True)
    o_ref[...] = (x * lax.rsqrt(ms + NORM_EPS) * g_ref[...]).astype(o_ref.dtype)


def rmsnorm(x, g, out_dtype, tm=256):
    m, d = x.shape
    tm = min(tm, m)
    return pl.pallas_call(
        _rmsnorm_kernel,
        out_shape=jax.ShapeDtypeStruct((m, d), out_dtype),
        grid=(m // tm,),
        in_specs=[pl.BlockSpec((tm, d), lambda i: (i, 0)),
                  pl.BlockSpec((1, d), lambda i: (0, 0))],
        out_specs=pl.BlockSpec((tm, d), lambda i: (i, 0)),
        compiler_params=_params(("parallel",)),
        name="rmsnorm",
    )(x, g.reshape(1, d))


def _store_normed(x, g_ref, hn_ref, ss_ref, col_tile):
    hn_ref[...] = (x * g_ref[...]).astype(hn_ref.dtype)
    partial = jnp.sum(x * x, axis=-1, keepdims=True)

    @pl.when(col_tile == 0)
    def _():
        ss_ref[...] = partial

    @pl.when(col_tile > 0)
    def _():
        ss_ref[...] += partial


def _row_rstd(ss_ref, width):
    return lax.rsqrt(ss_ref[...] * (1.0 / width) + NORM_EPS)


def _normed_outputs(m, n, tm, tn, index_map_2d, index_map_rows):
    shapes = [jax.ShapeDtypeStruct((m, n), bf16), jax.ShapeDtypeStruct((m, 1), f32)]
    specs = [pl.BlockSpec((tm, tn), index_map_2d), pl.BlockSpec((tm, 1), index_map_rows)]
    return shapes, specs


def _matmul_kernel(*refs, nk, has_res, has_row_ss, emits_normed, w_is_transposed):
    a_ref, w_ref = refs[:2]
    rest = list(refs[2:])
    res_ref = rest.pop(0) if has_res else None
    row_ss_ref = rest.pop(0) if has_row_ss else None
    gain_ref = rest.pop(0) if emits_normed else None
    o_ref = rest.pop(0)
    hn_ref, ss_ref = (rest.pop(0), rest.pop(0)) if emits_normed else (None, None)
    dot = _dot_nt if w_is_transposed else _dot

    def finish(out):
        o_ref[...] = out.astype(o_ref.dtype)
        if emits_normed:
            _store_normed(out, gain_ref, hn_ref, ss_ref, pl.program_id(1))

    if nk == 1:
        out = dot(a_ref[...], w_ref[...])
        if has_row_ss:
            out = out * _row_rstd(row_ss_ref, a_ref.shape[1])
        if has_res:
            out = out + res_ref[...]
        finish(out)
        return
    acc_ref, = rest
    k = pl.program_id(2)

    @pl.when(k == 0)
    def _():
        acc_ref[...] = res_ref[...] if has_res else jnp.zeros_like(acc_ref)

    acc_ref[...] += dot(a_ref[...], w_ref[...])

    @pl.when(k == nk - 1)
    def _():
        finish(acc_ref[...])


def matmul(a, w, layer, *, w_is_transposed=False, n=None, res=None, row_ss=None, norm_gain=None,
           out_dtype=f32, tm=1024, tn=512, tk=None):
    m, k_dim = a.shape
    if n is None:
        n = w.shape[1] if w_is_transposed else w.shape[2]
    tm, tn = _pick_tile(m, tm), _pick_tile(n, tn)
    tk = k_dim if tk is None else _pick_tile(k_dim, tk)
    nk = k_dim // tk

    def vmem_estimate(rows):
        blocks = rows * tk * a.dtype.itemsize + tk * tn * 4
        blocks += rows * tn * jnp.dtype(out_dtype).itemsize
        blocks += rows * tn * 4 * (res is not None)
        blocks += rows * LANES * 4 * (row_ss is not None)
        blocks += (rows * tn * 2 + rows * LANES * 4) * (norm_gain is not None)
        return 2 * blocks + rows * tn * 4 * (nk > 1)

    while vmem_estimate(tm) > VMEM_LIMIT_BYTES - MATMUL_TEMP_BYTES and tm % (2 * SUBLANES) == 0:
        tm //= 2
    assert m % tm == 0 and n % tn == 0 and k_dim % tk == 0, (a.shape, w.shape, tm, tn, tk)
    assert row_ss is None or nk == 1
    if w_is_transposed:
        w_spec = pl.BlockSpec((None, tn, tk), lambda i, j, k: (layer, j, k))
    else:
        w_spec = pl.BlockSpec((None, tk, tn), lambda i, j, k: (layer, k, j))
    in_specs = [pl.BlockSpec((tm, tk), lambda i, j, k: (i, k)), w_spec]
    args = [a, w]
    if res is not None:
        in_specs.append(pl.BlockSpec((tm, tn), lambda i, j, k: (i, j)))
        args.append(res)
    if row_ss is not None:
        in_specs.append(pl.BlockSpec((tm, 1), lambda i, j, k: (i, 0)))
        args.append(row_ss)
    out_shape = [jax.ShapeDtypeStruct((m, n), out_dtype)]
    out_specs = [pl.BlockSpec((tm, tn), lambda i, j, k: (i, j))]
    if norm_gain is not None:
        in_specs.append(pl.BlockSpec((1, tn), lambda i, j, k: (0, j)))
        args.append(norm_gain.reshape(1, n))
        shapes, specs = _normed_outputs(m, n, tm, tn, lambda i, j, k: (i, j), lambda i, j, k: (i, 0))
        out_shape += shapes
        out_specs += specs
    col_semantics = "parallel" if norm_gain is None else "arbitrary"
    outs = pl.pallas_call(
        functools.partial(_matmul_kernel, nk=nk, has_res=res is not None,
                          has_row_ss=row_ss is not None, emits_normed=norm_gain is not None,
                          w_is_transposed=w_is_transposed),
        out_shape=out_shape,
        grid=(m // tm, n // tn, nk),
        in_specs=in_specs,
        out_specs=out_specs,
        scratch_shapes=[pltpu.VMEM((tm, tn), f32)] if nk > 1 else [],
        compiler_params=_params(("parallel", col_semantics, "arbitrary")),
        name="matmul",
    )(*args)
    return outs[0] if norm_gain is None else (outs[0], (outs[1], outs[2]))


def _out_proj_kernel(mix_ref, xo_ref, w_ref, res_ref, gain_ref, o_ref, hn_ref, ss_ref):
    k_mix = mix_ref.shape[1]
    out = (res_ref[...] + _dot(mix_ref[...], w_ref[:k_mix, :])
           + _dot(xo_ref[...], w_ref[k_mix:, :]))
    o_ref[...] = out
    _store_normed(out, gain_ref, hn_ref, ss_ref, pl.program_id(1))


def _out_proj_tiles(m, k_total, n):
    for tm, tn in ((2048, 512), (2048, 256), (1024, 512), (1024, 256), (512, 256)):
        tm, tn = _pick_tile(m, tm), _pick_tile(n, tn)
        acts = 2 * tm * k_total * 2
        weights = 2 * k_total * tn * 4 + k_total * tn * 2
        tiles = 2 * tm * tn * (4 + 4 + 2) + 2 * tm * LANES * 4
        temps = 2 * tm * tn * 4
        if acts + weights + tiles + temps <= VMEM_LIMIT_BYTES - MATMUL_TEMP_BYTES:
            break
    return tm, tn


def out_proj(mix, xo, w, layer, res, norm_gain):
    m, k_mix = mix.shape
    k_xo = xo.shape[1]
    n = w.shape[2]
    tm, tn = _out_proj_tiles(m, k_mix + k_xo, n)
    assert w.shape[1] == k_mix + k_xo
    shapes, specs = _normed_outputs(m, n, tm, tn, lambda i, j: (i, j), lambda i, j: (i, 0))
    x, hn, ss = pl.pallas_call(
        _out_proj_kernel,
        out_shape=[jax.ShapeDtypeStruct((m, n), f32)] + shapes,
        grid=(m // tm, n // tn),
        in_specs=[pl.BlockSpec((tm, k_mix), lambda i, j: (i, 0)),
                  pl.BlockSpec((tm, k_xo), lambda i, j: (i, 0)),
                  pl.BlockSpec((None, k_mix + k_xo, tn), lambda i, j: (layer, 0, j)),
                  pl.BlockSpec((tm, tn), lambda i, j: (i, j)),
                  pl.BlockSpec((1, tn), lambda i, j: (0, j))],
        out_specs=[pl.BlockSpec((tm, tn), lambda i, j: (i, j))] + specs,
        compiler_params=_params(("parallel", "arbitrary")),
        name="out_proj",
    )(mix, xo, w, res, norm_gain.reshape(1, n))
    return x, (hn, ss)


def _ffn_up_kernel(a_ref, ss_ref, wg_ref, wu_ref, cw_ref, cb_ref, o_ref, wg_bf, wu_bf, tail_ref,
                   *, tiles_per_seq):
    i = pl.program_id(1)

    @pl.when(i == 0)
    def _():
        wg_bf[...] = wg_ref[...].astype(bf16)
        wu_bf[...] = wu_ref[...].astype(bf16)

    a = a_ref[...]
    rstd = _row_rstd(ss_ref, a.shape[1])
    gate = jnp.dot(a, wg_bf[...], preferred_element_type=f32) * rstd
    up = jnp.dot(a, wu_bf[...], preferred_element_type=f32) * rstd
    tm = gate.shape[0]

    tail = jnp.where(i % tiles_per_seq == 0, 0.0, tail_ref[...])
    tail_ref[...] = gate[tm - SUBLANES:, :]

    row = lax.broadcasted_iota(jnp.int32, gate.shape, 0)
    last, last2 = tail[SUBLANES - 1:SUBLANES, :], tail[SUBLANES - 2:SUBLANES - 1, :]
    g1 = jnp.where(row == 0, last, pltpu.roll(gate, 1, 0))
    g2 = jnp.where(row == 0, last2, jnp.where(row == 1, last, pltpu.roll(gate, 2, 0)))
    cw = cw_ref[...]
    conv = cb_ref[...] + cw[0:1, :] * g2 + cw[1:2, :] * g1 + cw[2:3, :] * gate
    o_ref[...] = (conv * _sigmoid(conv) * up).astype(o_ref.dtype)


def ffn_up(normed, w_up, conv_w, conv_b, layer, seq_len, *, tm=512, tn=512):
    h, row_ss = normed
    m, d = h.shape
    d_ff = w_up.shape[2] // 2
    tm, tn = _pick_tile(seq_len, tm), _pick_tile(d_ff, tn)
    assert m % tm == 0 and seq_len % tm == 0 and d_ff % tn == 0
    nj = d_ff // tn
    depth = conv_w.shape[0]
    return pl.pallas_call(
        functools.partial(_ffn_up_kernel, tiles_per_seq=seq_len // tm),
        out_shape=jax.ShapeDtypeStruct((m, d_ff), bf16),
        grid=(nj, m // tm),
        in_specs=[pl.BlockSpec((tm, d), lambda j, i: (i, 0)),
                  pl.BlockSpec((tm, 1), lambda j, i: (i, 0)),
                  pl.BlockSpec((None, d, tn), lambda j, i: (layer, 0, j)),
                  pl.BlockSpec((None, d, tn), lambda j, i: (layer, 0, j + nj)),
                  pl.BlockSpec((None, CONV_WIDTH, tn), lambda j, i: (layer, 0, j)),
                  pl.BlockSpec((None, 1, tn), lambda j, i: (layer, 0, j))],
        out_specs=pl.BlockSpec((tm, tn), lambda j, i: (i, j)),
        scratch_shapes=[pltpu.VMEM((d, tn), bf16), pltpu.VMEM((d, tn), bf16),
                        pltpu.VMEM((SUBLANES, tn), f32)],
        compiler_params=_params(("arbitrary", "arbitrary")),
        name="ffn_up",
    )(h, row_ss, w_up, w_up, conv_w, conv_b.reshape(depth, 1, d_ff))


def _rope_table_kernel(pos_ref, freq_ref, o_ref):
    ang = pos_ref[...].astype(f32) * freq_ref[...]
    lane = lax.broadcasted_iota(jnp.int32, ang.shape, 1)
    o_ref[0] = jnp.where(lane < ROT_DIM, jnp.cos(ang), 1.0)
    o_ref[1] = jnp.where(lane < ROT_DIM, jnp.sin(ang), 0.0)


def rope_tables(positions, tm=512):
    t = positions.size
    half = ROT_DIM // 2
    inv_freq = ROPE_THETA ** (-jnp.arange(half, dtype=f32) / half)
    freq = jnp.zeros((1, A_HEAD_DIM), f32).at[0, :ROT_DIM].set(jnp.tile(inv_freq, 2))
    tm = _pick_tile(t, tm)
    return pl.pallas_call(
        _rope_table_kernel,
        out_shape=jax.ShapeDtypeStruct((2, t, LANES), f32),
        grid=(t // tm,),
        in_specs=[pl.BlockSpec((tm, 1), lambda i: (i, 0)),
                  pl.BlockSpec((1, A_HEAD_DIM), lambda i: (0, 0))],
        out_specs=pl.BlockSpec((2, tm, LANES), lambda i: (0, i, 0)),
        compiler_params=_params(("parallel",)),
        name="rope_tables",
    )(positions.reshape(t, 1), freq)


def _rotate_half_matrix():
    half = ROT_DIM // 2
    src = lax.broadcasted_iota(jnp.int32, (A_HEAD_DIM, A_HEAD_DIM), 0)
    dst = lax.broadcasted_iota(jnp.int32, (A_HEAD_DIM, A_HEAD_DIM), 1)
    minus = (dst < half) & (src == dst + half)
    plus = (dst >= half) & (dst < ROT_DIM) & (src == dst - half)
    return jnp.where(minus, -1.0, jnp.where(plus, 1.0, 0.0)).astype(bf16)


def _rope(x, cos, sin, rot_mat):
    return x * cos + _split_dot(x, rot_mat) * sin


def _dilated_attn_kernel(*refs, dilations, n_backs):
    ng = len(dilations)
    q_refs, kc_refs, kp_refs = refs[0:ng], refs[ng:2 * ng], refs[2 * ng:3 * ng]
    vc_refs, vp_refs = refs[3 * ng:4 * ng], refs[4 * ng:5 * ng]
    tc_ref, tp_ref, o_ref = refs[5 * ng:5 * ng + 3]
    scratch = refs[5 * ng + 3:]
    o_scr, lse_scr, q_scr, k_scr, kp_scr = (scratch[i * ng:(i + 1) * ng] for i in range(5))

    blk = ATT_BLOCK
    span = q_refs[0].shape[0]
    first_span = pl.program_id(1) == 0
    qi = lax.broadcasted_iota(jnp.int32, (blk, 2 * blk), 0)
    col = lax.broadcasted_iota(jnp.int32, (blk, 2 * blk), 1)
    dist = qi + blk - col
    first_col = jnp.where(first_span, blk, 0)
    scale = A_HEAD_DIM ** -0.5

    cos_c, sin_c = tc_ref[0], tc_ref[1]
    rot_mat = _rotate_half_matrix()
    for g, d in enumerate(dilations):
        tail = pl.ds(span - d * blk, d * blk)
        q_scr[g][...] = _rope(q_refs[g][...], cos_c, sin_c, rot_mat) * scale
        k_scr[g][...] = _rope(kc_refs[g][...], cos_c, sin_c, rot_mat)
        kp_scr[g][...] = _rope(kp_refs[g][...], tp_ref[0, tail, :], tp_ref[1, tail, :], rot_mat)

    ones_kv = jnp.ones((2 * blk, A_HEAD_DIM), bf16)

    def rows(start, size, d):
        return pl.ds(start, size, stride=d) if d > 1 else pl.ds(start, size)

    def load_unit(g, d, r, j):
        span_g = d * blk
        cur = rows(j * span_g + r, blk, d)
        if j == 0:
            prev = rows(r, blk, d)
            k2 = jnp.concatenate([kp_scr[g][prev, :], k_scr[g][cur, :]], axis=0)
            v2 = jnp.concatenate([vp_refs[g][prev, :], vc_refs[g][cur, :]], axis=0)
        else:
            both = rows((j - 1) * span_g + r, 2 * blk, d)
            k2 = k_scr[g][both, :]
            v2 = vc_refs[g][both, :]
        return q_scr[g][cur, :].astype(bf16), k2.astype(bf16), v2.astype(bf16)

    for g, (d, n_back) in enumerate(zip(dilations, n_backs)):
        in_window = (dist >= 0) & (dist <= n_back)
        in_window_first = in_window & (col >= first_col)
        units = [(r, j) for r in range(d) for j in range(span // (d * blk))]
        for b0 in range(0, len(units), ATT_UNITS_PER_BATCH):
            batch_units = units[b0:b0 + ATT_UNITS_PER_BATCH]
            loaded = [load_unit(g, d, r, j) for r, j in batch_units]
            scores = [jnp.where(in_window_first if j == 0 else in_window, _dot_nt(q, k2), NEG_INF)
                      for (q, k2, _), (_, j) in zip(loaded, batch_units)]
            maxes = [jnp.max(s, axis=-1, keepdims=True) for s in scores]
            probs = [jnp.exp(s - m).astype(bf16) for s, m in zip(scores, maxes)]
            sums = [jnp.dot(p, ones_kv, preferred_element_type=f32) for p in probs]
            outs = [_dot(p, v2) * (1.0 / l) for p, (_, _, v2), l in zip(probs, loaded, sums)]
            for (r, j), o, m, l in zip(batch_units, outs, maxes, sums):
                cur = rows(j * d * blk + r, blk, d)
                o_scr[g][cur, :] = o
                lse_scr[g][cur, :] = m + jnp.log(l)

    lses = [ref[...] for ref in lse_scr]
    m = functools.reduce(jnp.maximum, lses)
    wts = [jnp.exp(x - m) for x in lses]
    num = sum(w * ref[...] for w, ref in zip(wts, o_scr))
    o_ref[...] = (num / sum(wts)).astype(o_ref.dtype)


def dilated_attn(p_in, tabs, batch, seq_len):
    t, ncol = p_in.shape
    ng = len(A_GROUPS)
    gw = ncol // (3 * ng + 1)
    heads = gw // A_HEAD_DIM
    dilations = tuple(d for _, d in A_GROUPS)
    n_backs = tuple(w // d for w, d in A_GROUPS)
    span = max(dilations) * ATT_BLOCK
    assert all(nb <= ATT_BLOCK for nb in n_backs) and seq_len % span == 0
    assert all(span % (d * ATT_BLOCK) == 0 for d in dilations)
    nspan = seq_len // span

    def cur(part, g):
        return pl.BlockSpec((span, A_HEAD_DIM),
                            lambda b, n, h: (b * nspan + n, (part * ng + g) * heads + h))

    def prev(part, g):
        per_span = span // (dilations[g] * ATT_BLOCK)
        return pl.BlockSpec(
            (dilations[g] * ATT_BLOCK, A_HEAD_DIM),
            lambda b, n, h: (jnp.maximum((b * nspan + n) * per_span - 1, 0), (part * ng + g) * heads + h))

    groups = range(ng)
    in_specs = ([cur(0, g) for g in groups] + [cur(1, g) for g in groups] + [prev(1, g) for g in groups]
                + [cur(2, g) for g in groups] + [prev(2, g) for g in groups]
                + [pl.BlockSpec((2, span, LANES), lambda b, n, h: (0, b * nspan + n, 0)),
                   pl.BlockSpec((2, span, LANES), lambda b, n, h: (0, jnp.maximum(b * nspan + n - 1, 0), 0))])
    return pl.pallas_call(
        functools.partial(_dilated_attn_kernel, dilations=dilations, n_backs=n_backs),
        out_shape=jax.ShapeDtypeStruct((t, gw), bf16),
        grid=(batch, nspan, heads),
        in_specs=in_specs,
        out_specs=pl.BlockSpec((span, A_HEAD_DIM), lambda b, n, h: (b * nspan + n, h)),
        scratch_shapes=([pltpu.VMEM((span, A_HEAD_DIM), f32)] * (4 * ng)
                        + [pltpu.VMEM((d * ATT_BLOCK, A_HEAD_DIM), f32) for d in dilations]),
        compiler_params=_params(("parallel", "parallel", "arbitrary")),
        name="dilated_attn",
    )(*([p_in] * (5 * ng)), tabs, tabs)


def _xattn_kernel(q_ref, k_ref, v_ref, o_ref, *, heads):
    hd = q_ref.shape[1] // heads
    scale = hd ** -0.5
    for h in range(heads):
        sl = slice(h * hd, (h + 1) * hd)
        s = _dot_nt(q_ref[:, sl], k_ref[:, sl]) * scale
        m = jnp.max(s, axis=-1, keepdims=True)
        p = jnp.exp(s - m)
        l = jnp.sum(p, axis=-1, keepdims=True)
        o_ref[:, sl] = _dot(p / l, v_ref[:, sl]).astype(o_ref.dtype)


def cross_attn(q_arr, q_col_block, mem_kv, batch, seq_len, tq=512):
    xw = mem_kv.shape[1] // 2
    mem_len = mem_kv.shape[0] // batch
    tq = _pick_tile(seq_len, tq)
    nq = seq_len // tq
    return pl.pallas_call(
        functools.partial(_xattn_kernel, heads=X_HEADS),
        out_shape=jax.ShapeDtypeStruct((batch * seq_len, xw), bf16),
        grid=(batch, nq),
        in_specs=[pl.BlockSpec((tq, xw), lambda b, i: (b * nq + i, q_col_block)),
                  pl.BlockSpec((mem_len, xw), lambda b, i: (b, 0)),
                  pl.BlockSpec((mem_len, xw), lambda b, i: (b, 1))],
        out_specs=pl.BlockSpec((tq, xw), lambda b, i: (b * nq + i, 0)),
        compiler_params=_params(("parallel", "parallel")),
        name="cross_attn",
    )(q_arr, mem_kv, mem_kv)


def _shift(x, last_rows, mu, seq_start):
    row = lax.broadcasted_iota(jnp.int32, x.shape, 0)
    last = jnp.where(seq_start, 0.0, last_rows[SUBLANES - 1:SUBLANES, :])
    prev = jnp.where(row == 0, last, pltpu.roll(x, 1, 0))
    return x + (prev - x) * mu


def _low_rank_acts_kernel(lo_ref, halo_ref, mu_ref, o_ref, *, tiles_per_seq, segments):
    seq_start = pl.program_id(0) % tiles_per_seq == 0
    low = _shift(lo_ref[...], halo_ref[...], mu_ref[...], seq_start)
    (w0_, w1_), (a0_, a1_), (v0_, v1_), (g0_, g1_) = segments
    o_ref[:, w0_:w1_] = jnp.tanh(low[:, w0_:w1_]).astype(o_ref.dtype)
    o_ref[:, a0_:v1_] = low[:, a0_:v1_].astype(o_ref.dtype)
    o_ref[:, g0_:g1_] = _sigmoid(low[:, g0_:g1_]).astype(o_ref.dtype)


def low_rank_acts(p_low, mu_low, seq_len, segments, tm=512):
    t, width = p_low.shape
    tm = _pick_tile(seq_len, tm)
    hb = tm // SUBLANES
    return pl.pallas_call(
        functools.partial(_low_rank_acts_kernel, tiles_per_seq=seq_len // tm, segments=segments),
        out_shape=jax.ShapeDtypeStruct((t, width), bf16),
        grid=(t // tm,),
        in_specs=[pl.BlockSpec((tm, width), lambda i: (i, 0)),
                  pl.BlockSpec((SUBLANES, width), lambda i: (jnp.maximum(i * hb - 1, 0), 0)),
                  pl.BlockSpec((1, width), lambda i: (0, 0))],
        out_specs=pl.BlockSpec((tm, width), lambda i: (i, 0)),
        compiler_params=_params(("parallel",)),
        name="low_rank_acts",
    )(p_low, p_low, mu_low)


def _rwkv_project_kernel(hn_ref, ss_ref, wr_ref, wk_ref, wv_ref, la_ref, vf_ref,
                         mu_r, mu_k, mu_v, w0, a0, v0, k_k, k_a, wup, aup, vup, gup,
                         r_o, lw_o, k_o, v_o, kk_o, b_o, g_o,
                         w_bf, tail_ref, *, tiles_per_seq, segments):
    i = pl.program_id(1)
    tn = wr_ref.shape[0]

    @pl.when(i == 0)
    def _():
        w_bf[0:tn, :] = wr_ref[...].astype(bf16)
        w_bf[tn:2 * tn, :] = wk_ref[...].astype(bf16)
        w_bf[2 * tn:3 * tn, :] = wv_ref[...].astype(bf16)

    seq_start = i % tiles_per_seq == 0
    a = hn_ref[...]
    rstd = _row_rstd(ss_ref, a.shape[1])
    tm = a.shape[0]

    (w0_, w1_), (a0_, a1_), (v0_, v1_), (g0_, g1_) = segments
    la = la_ref[...]
    xw = w0[...] + _dot(la[:, w0_:w1_], wup[...])
    lw_o[...] = -math.exp(-0.5) * _sigmoid(xw)
    ag = _sigmoid(a0[...] + _dot(la[:, a0_:a1_], aup[...]))
    v_gate = _sigmoid(v0[...] + _dot(la[:, v0_:v1_], vup[...]))
    g_o[...] = _dot(la[:, g0_:g1_], gup[...])

    raw = _dot_nt(a, w_bf[...]) * rstd
    tail = tail_ref[...]
    tail_ref[...] = raw[tm - SUBLANES:, :]

    def shifted(part, mu_ref):
        sl = slice(part * tn, (part + 1) * tn)
        return _shift(raw[:, sl], tail[:, sl], mu_ref[...], seq_start)

    r_o[...] = shifted(0, mu_r)

    k = shifted(1, mu_k)
    kk = k * k_k[...]
    ones = _head_block_ones(kk.shape[1], R_HEAD_DIM)
    norm = jnp.sqrt(_split_dot(kk * kk, ones))
    kk = kk / jnp.maximum(norm, L2_EPS)
    kk_o[...] = kk
    b_o[...] = kk * ag
    k_o[...] = k * (1.0 + (ag - 1.0) * k_a[...])

    v = shifted(2, mu_v)
    v_o[...] = v + (vf_ref[...] - v) * v_gate


def rwkv_project(normed, w_t, layer, low_acts, segments, p_a, mix_w, seq_len, mu, w0, a0, v0,
                 k_k, k_a, wup, aup, vup, gup, *, tm=512, tn=256):
    hn, row_ss = normed
    t, d = hn.shape
    tm, tn = _pick_tile(seq_len, tm), _pick_tile(mix_w, tn)
    assert seq_len % tm == 0 and mix_w % tn == 0 and tn % LANES == 0
    nc = mix_w // tn
    low_width = low_acts.shape[1]

    def w_rows(part):
        return pl.BlockSpec((None, tn, d), lambda j, i: (layer, part * nc + j, 0))

    def row(part=0):
        return pl.BlockSpec((1, tn), lambda j, i: (0, part * nc + j))

    def lora(ref):
        return pl.BlockSpec((ref.shape[0], tn), lambda j, i: (0, j))

    out_spec = pl.BlockSpec((tm, tn), lambda j, i: (i, j))
    mu_full = mu.reshape(1, -1)
    in_specs = [pl.BlockSpec((tm, d), lambda j, i: (i, 0)),
                pl.BlockSpec((tm, 1), lambda j, i: (i, 0)),
                w_rows(0), w_rows(1), w_rows(2),
                pl.BlockSpec((tm, low_width), lambda j, i: (i, 0)),
                pl.BlockSpec((tm, tn), lambda j, i: (i, 2 * nc + j)),
                row(0), row(1), row(2), row(), row(), row(), row(), row(),
                lora(wup), lora(aup), lora(vup), lora(gup)]
    return pl.pallas_call(
        functools.partial(_rwkv_project_kernel, tiles_per_seq=seq_len // tm, segments=segments),
        out_shape=(jax.ShapeDtypeStruct((t, mix_w), f32),) * 7,
        grid=(nc, t // tm),
        in_specs=in_specs,
        out_specs=(out_spec,) * 7,
        scratch_shapes=[pltpu.VMEM((3 * tn, d), bf16), pltpu.VMEM((SUBLANES, 3 * tn), f32)],
        compiler_params=_params(("arbitrary", "arbitrary")),
        name="rwkv_project",
    )(hn, row_ss, w_t, w_t, w_t, low_acts, p_a, mu_full, mu_full, mu_full,
      w0.reshape(1, -1), a0.reshape(1, -1), v0.reshape(1, -1), k_k.reshape(1, -1),
      k_a.reshape(1, -1), wup, aup, vup, gup)


def _rwkv_scan_kernel(r_ref, lw_ref, k_ref, v_ref, kk_ref, b_ref, g_ref, lng_ref, lnb_ref, rk_ref,
                      o_ref, state_ref):
    @pl.when(pl.program_id(2) == 0)
    def _():
        state_ref[...] = jnp.zeros_like(state_ref)

    c = SCAN_CHUNK
    n = 2 * c
    pairs = range(r_ref.shape[1] // LANES)
    cols = [slice(p * LANES, (p + 1) * LANES) for p in pairs]

    tri_r = lax.broadcasted_iota(jnp.int32, (c, c), 0)
    tri_c = lax.broadcasted_iota(jnp.int32, (c, c), 1)
    tri = jnp.where(tri_r >= tri_c, 1.0, 0.0)
    lw_all = lw_ref[...]
    lcum_all = jnp.dot(tri, lw_all, preferred_element_type=f32, precision=lax.Precision.HIGHEST)

    head0 = lax.broadcasted_iota(jnp.int32, (c, LANES), 1) < R_HEAD_DIM
    row = lax.broadcasted_iota(jnp.int32, (n, n), 0)
    col = lax.broadcasted_iota(jnp.int32, (n, n), 1)
    rin, cin = row & (c - 1), col & (c - 1)
    strict, incl = rin > cin, rin >= cin
    eye = jnp.where(row == col, 1.0, 0.0)

    def merge_mask(level):
        rb, cb = row >> level, col >> level
        return ((rb >> 1) == (cb >> 1)) & ((rb & 1) == 1) & ((cb & 1) == 0)

    def stack(x):
        return jnp.concatenate([jnp.where(head0, x, 0.0), jnp.where(head0, 0.0, x)], axis=0)

    ops = []
    for sl in cols:
        lcum, lw = lcum_all[:, sl], lw_all[:, sl]
        r, k, v, kk, b = r_ref[:, sl], k_ref[:, sl], v_ref[:, sl], kk_ref[:, sl], b_ref[:, sl]
        l_end = lcum[c - 1:c, :]
        e_in = jnp.exp(lcum)
        e_out = jnp.exp(-lcum)
        e_end = jnp.exp(l_end - lcum)
        ops.append(dict(
            a_t=stack(-kk * jnp.exp(lcum - lw)), r_t=stack(r * e_in),
            b_t=stack(b * e_out), k_t=stack(k * e_out),
            bk_h=jnp.concatenate([stack(b * e_end), stack(k * e_end)], axis=0),
            v_s=stack(v), decay=jnp.exp(l_end)))

    for o in ops:
        s1 = _dot_nt(jnp.concatenate([o["a_t"], o["r_t"]], axis=0),
                     jnp.concatenate([o["b_t"], o["k_t"]], axis=0))
        o["ab"] = jnp.where(strict, s1[:n, :n], 0.0)
        o["ak"] = jnp.where(strict, s1[:n, n:], 0.0)
        o["rbk"] = jnp.concatenate([jnp.where(incl, s1[n:, :n], 0.0),
                                    jnp.where(incl, s1[n:, n:], 0.0)], axis=1)

    mask = merge_mask(0)
    ts = [eye + jnp.where(mask, o["ab"], 0.0) for o in ops]
    for level in range(1, int(math.log2(c))):
        mask = merge_mask(level)
        zs = [_dot(jnp.where(mask, o["ab"], 0.0), t) for o, t in zip(ops, ts)]
        ts = [t + _dot(t, z) for t, z in zip(ts, zs)]

    states = [state_ref[p] for p in pairs]
    rhs = [_dot_nt(o["a_t"], s) + _dot(o["ak"], o["v_s"]) for o, s in zip(ops, states)]
    us = [_dot(t, x) for t, x in zip(ts, rhs)]
    uvs = [jnp.concatenate([u, o["v_s"]], axis=0) for u, o in zip(us, ops)]
    y_stacked = [_dot_nt(o["r_t"], s) + _dot(o["rbk"], uv) for o, s, uv in zip(ops, states, uvs)]
    for p, o, s, uv in zip(pairs, ops, states, uvs):
        state_ref[p] = s * o["decay"] + _dot_tn(uv, o["bk_h"])

    ones = _head_block_ones(LANES, R_HEAD_DIM)
    inv_n = 1.0 / R_HEAD_DIM
    ys = [y[:c, :] + y[c:, :] for y in y_stacked]
    mus = [_split_dot(y, ones) * inv_n for y in ys]
    ycs = [y - mu for y, mu in zip(ys, mus)]
    vars_ = [_split_dot(yc * yc, ones) * inv_n for yc in ycs]
    bonus = [_split_dot(r_ref[:, sl] * k_ref[:, sl] * rk_ref[:, sl], ones) for sl in cols]
    for sl, yc, var, bo in zip(cols, ycs, vars_, bonus):
        y = yc * lax.rsqrt(var + GN_EPS) * lng_ref[:, sl] + lnb_ref[:, sl]
        o_ref[:, sl] = ((y + bo * v_ref[:, sl]) * g_ref[:, sl]).astype(o_ref.dtype)


def rwkv_scan(r, lw, k, v, kk, b, g, ln_g, ln_b, r_k, batch, seq_len, *, wb=3072):
    t, mix_w = r.shape
    wb = _pick_tile(mix_w, wb)
    assert mix_w % wb == 0 and wb % LANES == 0 and seq_len % SCAN_CHUNK == 0
    nchunk = seq_len // SCAN_CHUNK
    tile = pl.BlockSpec((SCAN_CHUNK, wb), lambda bi, w, n: (bi * nchunk + n, w))
    row = pl.BlockSpec((1, wb), lambda bi, w, n: (0, w))
    return pl.pallas_call(
        _rwkv_scan_kernel,
        out_shape=jax.ShapeDtypeStruct((t, mix_w), bf16),
        grid=(batch, mix_w // wb, nchunk),
        in_specs=[tile] * 7 + [row] * 3,
        out_specs=tile,
        scratch_shapes=[pltpu.VMEM((wb // LANES, LANES, LANES), f32)],
        compiler_params=_params(("parallel", "parallel", "arbitrary")),
        name="rwkv_scan",
    )(r, lw, k, v, kk, b, g, ln_g.reshape(1, -1), ln_b.reshape(1, -1), r_k.reshape(1, -1))


def _round_up(n, multiple):
    return -(-n // multiple) * multiple


def _pack_segments(x, sizes, axis):
    parts, start = [], 0
    for s in sizes:
        seg = lax.slice_in_dim(x, start, start + s, axis=axis)
        pad = [(0, 0)] * x.ndim
        pad[axis] = (0, _round_up(s, LANES) - s)
        parts.append(jnp.pad(seg, pad))
        start += s
    return jnp.concatenate(parts, axis=axis)


def _pad_rows(w, start, total):
    return jnp.zeros((total, w.shape[1]), bf16).at[start:start + w.shape[0]].set(w.astype(bf16))


def kernel(x, mem, positions, mem_norm_g, mem_w_kv, a_norm_g, a_w_in, a_w_out, b_norm_g, b_w_in, b_shift_mu, b_w0, b_w_up, b_a0, b_a_up, b_v0, b_v_up, b_g_up, b_k_k, b_k_a, b_r_k, b_ln_g, b_ln_b, b_w_out, ffn_norm_g, ffn_w_up, ffn_conv_w, ffn_conv_b, ffn_w_down, final_norm_g):
    batch, seq_len, d_model = x.shape
    t = batch * seq_len
    depth = ffn_w_up.shape[0]
    xw = d_model // 4
    mix_w = d_model - xw
    x = x.reshape(t, d_model)

    mem_h = rmsnorm(mem.reshape(-1, d_model), mem_norm_g, bf16)
    mem_kv = matmul(mem_h, mem_w_kv[None], 0)
    tabs = rope_tables(positions)

    def mixer_gain(layer):
        return (a_norm_g if layer % 2 == 0 else b_norm_g)[layer // 2]

    p_a = None
    h, row_ss = rmsnorm(x, mixer_gain(0), bf16), None
    for i in range(depth):
        j = i // 2
        if i % 2 == 0:
            p = matmul(h, a_w_in, j, row_ss=row_ss, tm=IN_PROJ_ROWS)
            if p_a is None:
                p_a = p
            mix = dilated_attn(p, tabs, batch, seq_len)
            xo = cross_attn(p, 3 * mix_w // xw, mem_kv, batch, seq_len)
            w_out = a_w_out
        else:
            w_bt = jnp.swapaxes(b_w_in, 1, 2)
            ups = (b_w_up[j], b_a_up[j], b_v_up[j], b_g_up[j])
            sizes = [u.shape[0] for u in ups]
            low_w = sum(sizes)
            w_low_t = _pack_segments(w_bt[j, 3 * mix_w:3 * mix_w + low_w], sizes, 0)[None]
            mu_low = _pack_segments(b_shift_mu[j][None, 3 * mix_w:], sizes, 1)
            segments, start = [], 0
            for s in sizes:
                segments.append((start, start + _round_up(s, LANES)))
                start = segments[-1][1]
            p_low = matmul(h, w_low_t, 0, w_is_transposed=True, row_ss=row_ss, tn=start)
            low_acts = low_rank_acts(p_low, mu_low, seq_len, tuple(segments))
            cq = matmul(h, w_bt[j:j + 1, 3 * mix_w + low_w:], 0, w_is_transposed=True,
                        row_ss=row_ss, out_dtype=bf16)
            prep = rwkv_project(
                (h, row_ss), w_bt, j, low_acts, tuple(segments), p_a, mix_w, seq_len,
                b_shift_mu[j], b_w0[j], b_a0[j], b_v0[j], b_k_k[j], b_k_a[j],
                *[_pad_rows(u, 0, _round_up(u.shape[0], LANES)) for u in ups])
            mix = rwkv_scan(*prep, b_ln_g[j], b_ln_b[j], b_r_k[j], batch, seq_len)
            xo = cross_attn(cq, 0, mem_kv, batch, seq_len)
            w_out = b_w_out
        x, normed = out_proj(mix, xo, w_out, j, x, ffn_norm_g[i])
        act = ffn_up(normed, ffn_w_up, ffn_conv_w, ffn_conv_b, i, seq_len)
        if i + 1 < depth:
            x, (h, row_ss) = matmul(act, ffn_w_down, i, res=x, norm_gain=mixer_gain(i + 1),
                                    tn=1024, tk=2048)
        else:
            x = matmul(act, ffn_w_down, i, res=x, tn=1024, tk=2048)
    return rmsnorm(x, final_norm_g, f32).reshape(batch, seq_len, d_model)
```

```python
import functools
import math

import jax
import jax.numpy as jnp
from jax import lax
from jax.experimental import pallas as pl
from jax.experimental.pallas import tpu as pltpu

f32 = jnp.float32
bf16 = jnp.bfloat16

X_HEADS = 4
A_HEAD_DIM = 128
A_GROUPS = ((128, 1), (512, 4), (2048, 16))
ATT_BLOCK = 128
ROPE_ROWS = 256
ATT_UNITS_PER_BATCH = 8
ROT_DIM = A_HEAD_DIM // 4
ROPE_THETA = 500000.0
R_HEAD_DIM = 64
CONV_WIDTH = 3
NORM_EPS = 1e-6
GN_EPS = 64e-5
L2_EPS = 1e-12
NEG_INF = -1e30

V7X_VMEM_BYTES = 64 * 2**20
VMEM_LIMIT_BYTES = V7X_VMEM_BYTES - 4 * 2**20
SUBLANES = 8
LANES = 128
IN_PROJ_ROWS = 2048
MATMUL_TEMP_BYTES = 3 * 2**20

SCAN_CHUNK = LANES // 2


def _pick_tile(n, preferred):
    if n % preferred == 0:
        return preferred
    g = math.gcd(n, preferred)
    return g if g % LANES == 0 else n


def _params(semantics):
    return pltpu.CompilerParams(dimension_semantics=semantics,
                                vmem_limit_bytes=VMEM_LIMIT_BYTES)


def _dot(a, b):
    return jnp.dot(a.astype(bf16), b.astype(bf16), preferred_element_type=f32)


def _dot_nt(a, b):
    return lax.dot_general(a.astype(bf16), b.astype(bf16), (((1,), (1,)), ((), ())),
                           preferred_element_type=f32)


def _dot_tn(a, b):
    return lax.dot_general(a.astype(bf16), b.astype(bf16), (((0,), (0,)), ((), ())),
                           preferred_element_type=f32)


def _sigmoid(x):
    return 1.0 / (1.0 + jnp.exp(-x))


def _split_dot(x, ones_mat):
    hi = x.astype(bf16)
    lo = (x - hi.astype(f32)).astype(bf16)
    rows = x.shape[0]
    both = jnp.dot(jnp.concatenate([hi, lo], axis=0), ones_mat, preferred_element_type=f32)
    return both[:rows] + both[rows:]


def _rows_before(x, tail, n):
    rolled = pltpu.roll(x, n, 0)
    row = lax.broadcasted_iota(jnp.int32, tail.shape, 0)
    top = jnp.where(row < n, pltpu.roll(tail, n, 0), rolled[:SUBLANES, :])
    return jnp.concatenate([top, rolled[SUBLANES:, :]], axis=0)


def _head_block_ones(width, head_dim):
    shift = int(math.log2(head_dim))
    r = lax.broadcasted_iota(jnp.int32, (width, width), 0) >> shift
    c = lax.broadcasted_iota(jnp.int32, (width, width), 1) >> shift
    return jnp.where(r == c, 1.0, 0.0).astype(bf16)


def _rmsnorm_kernel(x_ref, g_ref, o_ref):
    x = x_ref[...].astype(f32)
    ms = jnp.mean(x * x, axis=-1, keepdims=True)
    o_ref[...] = (x * lax.rsqrt(ms + NORM_EPS) * g_ref[...]).astype(o_ref.dtype)


def rmsnorm(x, g, out_dtype, tm=256):
    m, d = x.shape
    tm = min(tm, m)
    return pl.pallas_call(
        _rmsnorm_kernel,
        out_shape=jax.ShapeDtypeStruct((m, d), out_dtype),
        grid=(m // tm,),
        in_specs=[pl.BlockSpec((tm, d), lambda i: (i, 0)),
                  pl.BlockSpec((1, d), lambda i: (0, 0))],
        out_specs=pl.BlockSpec((tm, d), lambda i: (i, 0)),
        compiler_params=_params(("parallel",)),
        name="rmsnorm",
    )(x, g.reshape(1, d))


def _store_normed(x, g_ref, hn_ref, ss_ref, col_tile):
    hn_ref[...] = (x * g_ref[...]).astype(hn_ref.dtype)
    partial = jnp.sum(x * x, axis=-1, keepdims=True)

    @pl.when(col_tile == 0)
    def _():
        ss_ref[...] = partial

    @pl.when(col_tile > 0)
    def _():
        ss_ref[...] += partial


def _row_rstd(ss_ref, width):
    return lax.rsqrt(ss_ref[...] * (1.0 / width) + NORM_EPS)


def _normed_outputs(m, n, tm, tn, index_map_2d, index_map_rows):
    shapes = [jax.ShapeDtypeStruct((m, n), bf16), jax.ShapeDtypeStruct((m, 1), f32)]
    specs = [pl.BlockSpec((tm, tn), index_map_2d), pl.BlockSpec((tm, 1), index_map_rows)]
    return shapes, specs


def _matmul_kernel(*refs, nk, has_res, has_row_ss, emits_normed, w_is_transposed):
    a_ref, w_ref = refs[:2]
    rest = list(refs[2:])
    res_ref = rest.pop(0) if has_res else None
    row_ss_ref = rest.pop(0) if has_row_ss else None
    gain_ref = rest.pop(0) if emits_normed else None
    o_ref = rest.pop(0)
    hn_ref, ss_ref = (rest.pop(0), rest.pop(0)) if emits_normed else (None, None)
    dot = _dot_nt if w_is_transposed else _dot

    def finish(out):
        o_ref[...] = out.astype(o_ref.dtype)
        if emits_normed:
            _store_normed(out, gain_ref, hn_ref, ss_ref, pl.program_id(1))

    if nk == 1:
        out = dot(a_ref[...], w_ref[...])
        if has_row_ss:
            out = out * _row_rstd(row_ss_ref, a_ref.shape[1])
        if has_res:
            out = out + res_ref[...]
        finish(out)
        return
    acc_ref, = rest
    k = pl.program_id(2)

    @pl.when(k == 0)
    def _():
        acc_ref[...] = res_ref[...] if has_res else jnp.zeros_like(acc_ref)

    acc_ref[...] += dot(a_ref[...], w_ref[...])

    @pl.when(k == nk - 1)
    def _():
        finish(acc_ref[...])


def matmul(a, w, layer, *, w_is_transposed=False, n=None, res=None, row_ss=None, norm_gain=None,
           out_dtype=f32, tm=1024, tn=512, tk=None):
    m, k_dim = a.shape
    if n is None:
        n = w.shape[1] if w_is_transposed else w.shape[2]
    tm, tn = _pick_tile(m, tm), _pick_tile(n, tn)
    tk = k_dim if tk is None else _pick_tile(k_dim, tk)
    nk = k_dim // tk

    def vmem_estimate(rows):
        blocks = rows * tk * a.dtype.itemsize + tk * tn * 4
        blocks += rows * tn * jnp.dtype(out_dtype).itemsize
        blocks += rows * tn * 4 * (res is not None)
        blocks += rows * LANES * 4 * (row_ss is not None)
        blocks += (rows * tn * 2 + rows * LANES * 4) * (norm_gain is not None)
        return 2 * blocks + rows * tn * 4 * (nk > 1)

    while vmem_estimate(tm) > VMEM_LIMIT_BYTES - MATMUL_TEMP_BYTES and tm % (2 * SUBLANES) == 0:
        tm //= 2
    assert m % tm == 0 and n % tn == 0 and k_dim % tk == 0, (a.shape, w.shape, tm, tn, tk)
    assert row_ss is None or nk == 1
    if w_is_transposed:
        w_spec = pl.BlockSpec((None, tn, tk), lambda i, j, k: (layer, j, k))
    else:
        w_spec = pl.BlockSpec((None, tk, tn), lambda i, j, k: (layer, k, j))
    in_specs = [pl.BlockSpec((tm, tk), lambda i, j, k: (i, k)), w_spec]
    args = [a, w]
    if res is not None:
        in_specs.append(pl.BlockSpec((tm, tn), lambda i, j, k: (i, j)))
        args.append(res)
    if row_ss is not None:
        in_specs.append(pl.BlockSpec((tm, 1), lambda i, j, k: (i, 0)))
        args.append(row_ss)
    out_shape = [jax.ShapeDtypeStruct((m, n), out_dtype)]
    out_specs = [pl.BlockSpec((tm, tn), lambda i, j, k: (i, j))]
    if norm_gain is not None:
        in_specs.append(pl.BlockSpec((1, tn), lambda i, j, k: (0, j)))
        args.append(norm_gain.reshape(1, n))
        shapes, specs = _normed_outputs(m, n, tm, tn, lambda i, j, k: (i, j), lambda i, j, k: (i, 0))
        out_shape += shapes
        out_specs += specs
    col_semantics = "parallel" if norm_gain is None else "arbitrary"
    outs = pl.pallas_call(
        functools.partial(_matmul_kernel, nk=nk, has_res=res is not None,
                          has_row_ss=row_ss is not None, emits_normed=norm_gain is not None,
                          w_is_transposed=w_is_transposed),
        out_shape=out_shape,
        grid=(m // tm, n // tn, nk),
        in_specs=in_specs,
        out_specs=out_specs,
        scratch_shapes=[pltpu.VMEM((tm, tn), f32)] if nk > 1 else [],
        compiler_params=_params(("parallel", col_semantics, "arbitrary")),
        name="matmul",
    )(*args)
    return outs[0] if norm_gain is None else (outs[0], (outs[1], outs[2]))


def _out_proj_kernel(mix_ref, xo_ref, w_ref, res_ref, gain_ref, o_ref, hn_ref, ss_ref):
    k_mix = mix_ref.shape[1]
    out = (res_ref[...] + _dot(mix_ref[...], w_ref[:k_mix, :])
           + _dot(xo_ref[...], w_ref[k_mix:, :]))
    o_ref[...] = out
    _store_normed(out, gain_ref, hn_ref, ss_ref, pl.program_id(1))


def _out_proj_tiles(m, k_total, n):
    for tm, tn in ((2048, 512), (2048, 256), (1024, 512), (1024, 256), (512, 256)):
        tm, tn = _pick_tile(m, tm), _pick_tile(n, tn)
        acts = 2 * tm * k_total * 2
        weights = 2 * k_total * tn * 4 + k_total * tn * 2
        tiles = 2 * tm * tn * (4 + 4 + 2) + 2 * tm * LANES * 4
        temps = 2 * tm * tn * 4
        if acts + weights + tiles + temps <= VMEM_LIMIT_BYTES - MATMUL_TEMP_BYTES:
            break
    return tm, tn


def out_proj(mix, xo, w, layer, res, norm_gain):
    m, k_mix = mix.shape
    k_xo = xo.shape[1]
    n = w.shape[2]
    tm, tn = _out_proj_tiles(m, k_mix + k_xo, n)
    assert w.shape[1] == k_mix + k_xo
    shapes, specs = _normed_outputs(m, n, tm, tn, lambda i, j: (i, j), lambda i, j: (i, 0))
    x, hn, ss = pl.pallas_call(
        _out_proj_kernel,
        out_shape=[jax.ShapeDtypeStruct((m, n), f32)] + shapes,
        grid=(m // tm, n // tn),
        in_specs=[pl.BlockSpec((tm, k_mix), lambda i, j: (i, 0)),
                  pl.BlockSpec((tm, k_xo), lambda i, j: (i, 0)),
                  pl.BlockSpec((None, k_mix + k_xo, tn), lambda i, j: (layer, 0, j)),
                  pl.BlockSpec((tm, tn), lambda i, j: (i, j)),
                  pl.BlockSpec((1, tn), lambda i, j: (0, j))],
        out_specs=[pl.BlockSpec((tm, tn), lambda i, j: (i, j))] + specs,
        compiler_params=_params(("parallel", "arbitrary")),
        name="out_proj",
    )(mix, xo, w, res, norm_gain.reshape(1, n))
    return x, (hn, ss)


def _ffn_up_kernel(a_ref, ss_ref, wg_ref, wu_ref, cw_ref, cb_ref, o_ref, wg_bf, wu_bf, tail_ref,
                   *, tiles_per_seq):
    i = pl.program_id(1)

    @pl.when(i == 0)
    def _():
        wg_bf[...] = wg_ref[...].astype(bf16)
        wu_bf[...] = wu_ref[...].astype(bf16)

    a = a_ref[...]
    rstd = _row_rstd(ss_ref, a.shape[1])
    gate = jnp.dot(a, wg_bf[...], preferred_element_type=f32) * rstd
    up = jnp.dot(a, wu_bf[...], preferred_element_type=f32) * rstd
    tm = gate.shape[0]

    tail = jnp.where(i % tiles_per_seq == 0, 0.0, tail_ref[...])
    tail_ref[...] = gate[tm - SUBLANES:, :]

    cw = cw_ref[...]
    conv = (cb_ref[...] + cw[0:1, :] * _rows_before(gate, tail, 2)
            + cw[1:2, :] * _rows_before(gate, tail, 1) + cw[2:3, :] * gate)
    o_ref[...] = (conv * _sigmoid(conv) * up).astype(o_ref.dtype)


def ffn_up(normed, w_up, conv_w, conv_b, layer, seq_len, *, tm=512, tn=512):
    h, row_ss = normed
    m, d = h.shape
    d_ff = w_up.shape[2] // 2
    tm, tn = _pick_tile(seq_len, tm), _pick_tile(d_ff, tn)
    assert m % tm == 0 and seq_len % tm == 0 and d_ff % tn == 0
    nj = d_ff // tn
    depth = conv_w.shape[0]
    return pl.pallas_call(
        functools.partial(_ffn_up_kernel, tiles_per_seq=seq_len // tm),
        out_shape=jax.ShapeDtypeStruct((m, d_ff), bf16),
        grid=(nj, m // tm),
        in_specs=[pl.BlockSpec((tm, d), lambda j, i: (i, 0)),
                  pl.BlockSpec((tm, 1), lambda j, i: (i, 0)),
                  pl.BlockSpec((None, d, tn), lambda j, i: (layer, 0, j)),
                  pl.BlockSpec((None, d, tn), lambda j, i: (layer, 0, j + nj)),
                  pl.BlockSpec((None, CONV_WIDTH, tn), lambda j, i: (layer, 0, j)),
                  pl.BlockSpec((None, 1, tn), lambda j, i: (layer, 0, j))],
        out_specs=pl.BlockSpec((tm, tn), lambda j, i: (i, j)),
        scratch_shapes=[pltpu.VMEM((d, tn), bf16), pltpu.VMEM((d, tn), bf16),
                        pltpu.VMEM((SUBLANES, tn), f32)],
        compiler_params=_params(("arbitrary", "arbitrary")),
        name="ffn_up",
    )(h, row_ss, w_up, w_up, conv_w, conv_b.reshape(depth, 1, d_ff))


def _rope_table_kernel(pos_ref, freq_ref, o_ref):
    ang = pos_ref[...].astype(f32) * freq_ref[...]
    lane = lax.broadcasted_iota(jnp.int32, ang.shape, 1)
    o_ref[0] = jnp.where(lane < ROT_DIM, jnp.cos(ang), 1.0)
    o_ref[1] = jnp.where(lane < ROT_DIM, jnp.sin(ang), 0.0)


def rope_tables(positions, tm=512):
    t = positions.size
    half = ROT_DIM // 2
    inv_freq = ROPE_THETA ** (-jnp.arange(half, dtype=f32) / half)
    freq = jnp.zeros((1, A_HEAD_DIM), f32).at[0, :ROT_DIM].set(jnp.tile(inv_freq, 2))
    tm = _pick_tile(t, tm)
    return pl.pallas_call(
        _rope_table_kernel,
        out_shape=jax.ShapeDtypeStruct((2, t, LANES), f32),
        grid=(t // tm,),
        in_specs=[pl.BlockSpec((tm, 1), lambda i: (i, 0)),
                  pl.BlockSpec((1, A_HEAD_DIM), lambda i: (0, 0))],
        out_specs=pl.BlockSpec((2, tm, LANES), lambda i: (0, i, 0)),
        compiler_params=_params(("parallel",)),
        name="rope_tables",
    )(positions.reshape(t, 1), freq)


def _rotate_half_matrix():
    half = ROT_DIM // 2
    src = lax.broadcasted_iota(jnp.int32, (A_HEAD_DIM, A_HEAD_DIM), 0)
    dst = lax.broadcasted_iota(jnp.int32, (A_HEAD_DIM, A_HEAD_DIM), 1)
    minus = (dst < half) & (src == dst + half)
    plus = (dst >= half) & (dst < ROT_DIM) & (src == dst - half)
    return jnp.where(minus, -1.0, jnp.where(plus, 1.0, 0.0)).astype(bf16)


def _rope(x, cos, sin, rot_mat):
    return x * cos + _split_dot(x, rot_mat) * sin


def _dilated_attn_kernel(*refs, dilations, n_backs):
    ng = len(dilations)
    q_refs, kc_refs, kp_refs = refs[0:ng], refs[ng:2 * ng], refs[2 * ng:3 * ng]
    vc_refs, vp_refs = refs[3 * ng:4 * ng], refs[4 * ng:5 * ng]
    tc_ref, tp_ref, o_ref = refs[5 * ng:5 * ng + 3]
    scratch = refs[5 * ng + 3:]
    o_scr, lse_scr, q_scr, k_scr, kp_scr = (scratch[i * ng:(i + 1) * ng] for i in range(5))

    blk = ATT_BLOCK
    span = q_refs[0].shape[0]
    first_span = pl.program_id(1) == 0
    qi = lax.broadcasted_iota(jnp.int32, (blk, 2 * blk), 0)
    col = lax.broadcasted_iota(jnp.int32, (blk, 2 * blk), 1)
    dist = qi + blk - col
    first_col = jnp.where(first_span, blk, 0)
    scale = A_HEAD_DIM ** -0.5

    rot_mat = _rotate_half_matrix()
    for g, d in enumerate(dilations):
        tail = pl.ds(span - d * blk, d * blk)
        kp_scr[g][...] = _rope(kp_refs[g][...], tp_ref[0, tail, :], tp_ref[1, tail, :], rot_mat)
        for c0 in range(0, span, ROPE_ROWS):
            rows_c = pl.ds(c0, ROPE_ROWS)
            cos_c, sin_c = tc_ref[0, rows_c, :], tc_ref[1, rows_c, :]
            q_scr[g][rows_c, :] = _rope(q_refs[g][rows_c, :], cos_c, sin_c, rot_mat) * scale
            k_scr[g][rows_c, :] = _rope(kc_refs[g][rows_c, :], cos_c, sin_c, rot_mat)

    ones_kv = jnp.ones((2 * blk, A_HEAD_DIM), bf16)

    def rows(start, size, d):
        return pl.ds(start, size, stride=d) if d > 1 else pl.ds(start, size)

    def load_unit(g, d, r, j):
        span_g = d * blk
        cur = rows(j * span_g + r, blk, d)
        if j == 0:
            prev = rows(r, blk, d)
            k2 = jnp.concatenate([kp_scr[g][prev, :], k_scr[g][cur, :]], axis=0)
            v2 = jnp.concatenate([vp_refs[g][prev, :], vc_refs[g][cur, :]], axis=0)
        else:
            both = rows((j - 1) * span_g + r, 2 * blk, d)
            k2 = k_scr[g][both, :]
            v2 = vc_refs[g][both, :]
        return q_scr[g][cur, :].astype(bf16), k2.astype(bf16), v2.astype(bf16)

    for g, (d, n_back) in enumerate(zip(dilations, n_backs)):
        in_window = (dist >= 0) & (dist <= n_back)
        in_window_first = in_window & (col >= first_col)
        units = [(r, j) for r in range(d) for j in range(span // (d * blk))]
        for b0 in range(0, len(units), ATT_UNITS_PER_BATCH):
            batch_units = units[b0:b0 + ATT_UNITS_PER_BATCH]
            loaded = [load_unit(g, d, r, j) for r, j in batch_units]
            scores = [jnp.where(in_window_first if j == 0 else in_window, _dot_nt(q, k2), NEG_INF)
                      for (q, k2, _), (_, j) in zip(loaded, batch_units)]
            maxes = [jnp.max(s, axis=-1, keepdims=True) for s in scores]
            probs = [jnp.exp(s - m).astype(bf16) for s, m in zip(scores, maxes)]
            sums = [jnp.dot(p, ones_kv, preferred_element_type=f32) for p in probs]
            outs = [_dot(p, v2) * (1.0 / l) for p, (_, _, v2), l in zip(probs, loaded, sums)]
            for (r, j), o, m, l in zip(batch_units, outs, maxes, sums):
                cur = rows(j * d * blk + r, blk, d)
                o_scr[g][cur, :] = o
                lse_scr[g][cur, :] = m + jnp.log(l)

    for c0 in range(0, span, ROPE_ROWS):
        rows_c = pl.ds(c0, ROPE_ROWS)
        lses = [ref[rows_c, :] for ref in lse_scr]
        m = functools.reduce(jnp.maximum, lses)
        wts = [jnp.exp(x - m) for x in lses]
        num = sum(w * ref[rows_c, :] for w, ref in zip(wts, o_scr))
        o_ref[rows_c, :] = (num / sum(wts)).astype(o_ref.dtype)


def dilated_attn(p_in, tabs, batch, seq_len):
    t, ncol = p_in.shape
    ng = len(A_GROUPS)
    gw = ncol // (3 * ng + 1)
    heads = gw // A_HEAD_DIM
    dilations = tuple(d for _, d in A_GROUPS)
    n_backs = tuple(w // d for w, d in A_GROUPS)
    span = max(dilations) * ATT_BLOCK
    assert all(nb <= ATT_BLOCK for nb in n_backs) and seq_len % span == 0
    assert all(span % (d * ATT_BLOCK) == 0 for d in dilations)
    nspan = seq_len // span

    def cur(part, g):
        return pl.BlockSpec((span, A_HEAD_DIM),
                            lambda b, n, h: (b * nspan + n, (part * ng + g) * heads + h))

    def prev(part, g):
        per_span = span // (dilations[g] * ATT_BLOCK)
        return pl.BlockSpec(
            (dilations[g] * ATT_BLOCK, A_HEAD_DIM),
            lambda b, n, h: (jnp.maximum((b * nspan + n) * per_span - 1, 0), (part * ng + g) * heads + h))

    groups = range(ng)
    in_specs = ([cur(0, g) for g in groups] + [cur(1, g) for g in groups] + [prev(1, g) for g in groups]
                + [cur(2, g) for g in groups] + [prev(2, g) for g in groups]
                + [pl.BlockSpec((2, span, LANES), lambda b, n, h: (0, b * nspan + n, 0)),
                   pl.BlockSpec((2, span, LANES), lambda b, n, h: (0, jnp.maximum(b * nspan + n - 1, 0), 0))])
    return pl.pallas_call(
        functools.partial(_dilated_attn_kernel, dilations=dilations, n_backs=n_backs),
        out_shape=jax.ShapeDtypeStruct((t, gw), bf16),
        grid=(batch, nspan, heads),
        in_specs=in_specs,
        out_specs=pl.BlockSpec((span, A_HEAD_DIM), lambda b, n, h: (b * nspan + n, h)),
        scratch_shapes=([pltpu.VMEM((span, A_HEAD_DIM), f32)] * (4 * ng)
                        + [pltpu.VMEM((d * ATT_BLOCK, A_HEAD_DIM), f32) for d in dilations]),
        compiler_params=_params(("parallel", "parallel", "arbitrary")),
        name="dilated_attn",
    )(*([p_in] * (5 * ng)), tabs, tabs)


def _xattn_kernel(q_ref, k_ref, v_ref, o_ref, *, heads):
    hd = q_ref.shape[1] // heads
    scale = hd ** -0.5
    for h in range(heads):
        sl = slice(h * hd, (h + 1) * hd)
        s = _dot_nt(q_ref[:, sl], k_ref[:, sl]) * scale
        m = jnp.max(s, axis=-1, keepdims=True)
        p = jnp.exp(s - m)
        l = jnp.sum(p, axis=-1, keepdims=True)
        o_ref[:, sl] = _dot(p / l, v_ref[:, sl]).astype(o_ref.dtype)


def cross_attn(q_arr, q_col_block, mem_kv, batch, seq_len, tq=512):
    xw = mem_kv.shape[1] // 2
    mem_len = mem_kv.shape[0] // batch
    tq = _pick_tile(seq_len, tq)
    nq = seq_len // tq
    return pl.pallas_call(
        functools.partial(_xattn_kernel, heads=X_HEADS),
        out_shape=jax.ShapeDtypeStruct((batch * seq_len, xw), bf16),
        grid=(batch, nq),
        in_specs=[pl.BlockSpec((tq, xw), lambda b, i: (b * nq + i, q_col_block)),
                  pl.BlockSpec((mem_len, xw), lambda b, i: (b, 0)),
                  pl.BlockSpec((mem_len, xw), lambda b, i: (b, 1))],
        out_specs=pl.BlockSpec((tq, xw), lambda b, i: (b * nq + i, 0)),
        compiler_params=_params(("parallel", "parallel")),
        name="cross_attn",
    )(q_arr, mem_kv, mem_kv)


def _shift(x, last_rows, mu, seq_start):
    prev = _rows_before(x, jnp.where(seq_start, 0.0, last_rows), 1)
    return x + (prev - x) * mu


def _low_rank_acts_kernel(lo_ref, halo_ref, mu_ref, o_ref, *, tiles_per_seq, segments):
    seq_start = pl.program_id(0) % tiles_per_seq == 0
    low = _shift(lo_ref[...], halo_ref[...], mu_ref[...], seq_start)
    (w0_, w1_), (a0_, a1_), (v0_, v1_), (g0_, g1_) = segments
    o_ref[:, w0_:w1_] = jnp.tanh(low[:, w0_:w1_]).astype(o_ref.dtype)
    o_ref[:, a0_:v1_] = low[:, a0_:v1_].astype(o_ref.dtype)
    o_ref[:, g0_:g1_] = _sigmoid(low[:, g0_:g1_]).astype(o_ref.dtype)


def low_rank_acts(p_low, mu_low, seq_len, segments, tm=512):
    t, width = p_low.shape
    tm = _pick_tile(seq_len, tm)
    hb = tm // SUBLANES
    return pl.pallas_call(
        functools.partial(_low_rank_acts_kernel, tiles_per_seq=seq_len // tm, segments=segments),
        out_shape=jax.ShapeDtypeStruct((t, width), bf16),
        grid=(t // tm,),
        in_specs=[pl.BlockSpec((tm, width), lambda i: (i, 0)),
                  pl.BlockSpec((SUBLANES, width), lambda i: (jnp.maximum(i * hb - 1, 0), 0)),
                  pl.BlockSpec((1, width), lambda i: (0, 0))],
        out_specs=pl.BlockSpec((tm, width), lambda i: (i, 0)),
        compiler_params=_params(("parallel",)),
        name="low_rank_acts",
    )(p_low, p_low, mu_low)


def _rwkv_project_kernel(hn_ref, ss_ref, wr_ref, wk_ref, wv_ref, la_ref, vf_ref,
                         mu_r, mu_k, mu_v, w0, a0, v0, k_k, k_a, wup, aup, vup, gup,
                         r_o, lw_o, k_o, v_o, kk_o, b_o, g_o,
                         w_bf, tail_ref, *, tiles_per_seq, segments):
    i = pl.program_id(1)
    tn = wr_ref.shape[0]

    @pl.when(i == 0)
    def _():
        w_bf[0:tn, :] = wr_ref[...].astype(bf16)
        w_bf[tn:2 * tn, :] = wk_ref[...].astype(bf16)
        w_bf[2 * tn:3 * tn, :] = wv_ref[...].astype(bf16)

    seq_start = i % tiles_per_seq == 0
    a = hn_ref[...]
    rstd = _row_rstd(ss_ref, a.shape[1])
    tm = a.shape[0]

    (w0_, w1_), (a0_, a1_), (v0_, v1_), (g0_, g1_) = segments
    la = la_ref[...]
    xw = w0[...] + _dot(la[:, w0_:w1_], wup[...])
    lw_o[...] = -math.exp(-0.5) * _sigmoid(xw)
    ag = _sigmoid(a0[...] + _dot(la[:, a0_:a1_], aup[...]))
    v_gate = _sigmoid(v0[...] + _dot(la[:, v0_:v1_], vup[...]))
    g_o[...] = _dot(la[:, g0_:g1_], gup[...])

    raw = _dot_nt(a, w_bf[...]) * rstd
    tail = tail_ref[...]
    tail_ref[...] = raw[tm - SUBLANES:, :]

    def shifted(part, mu_ref):
        sl = slice(part * tn, (part + 1) * tn)
        return _shift(raw[:, sl], tail[:, sl], mu_ref[...], seq_start)

    r_o[...] = shifted(0, mu_r)

    k = shifted(1, mu_k)
    kk = k * k_k[...]
    ones = _head_block_ones(kk.shape[1], R_HEAD_DIM)
    norm = jnp.sqrt(_split_dot(kk * kk, ones))
    kk = kk / jnp.maximum(norm, L2_EPS)
    kk_o[...] = kk
    b_o[...] = kk * ag
    k_o[...] = k * (1.0 + (ag - 1.0) * k_a[...])

    v = shifted(2, mu_v)
    v_o[...] = v + (vf_ref[...] - v) * v_gate


def rwkv_project(normed, w_t, layer, low_acts, segments, p_a, mix_w, seq_len, mu, w0, a0, v0,
                 k_k, k_a, wup, aup, vup, gup, *, tm=512, tn=256):
    hn, row_ss = normed
    t, d = hn.shape
    tm, tn = _pick_tile(seq_len, tm), _pick_tile(mix_w, tn)
    assert seq_len % tm == 0 and mix_w % tn == 0 and tn % LANES == 0
    nc = mix_w // tn
    low_width = low_acts.shape[1]

    def w_rows(part):
        return pl.BlockSpec((None, tn, d), lambda j, i: (layer, part * nc + j, 0))

    def row(part=0):
        return pl.BlockSpec((1, tn), lambda j, i: (0, part * nc + j))

    def lora(ref):
        return pl.BlockSpec((ref.shape[0], tn), lambda j, i: (0, j))

    out_spec = pl.BlockSpec((tm, tn), lambda j, i: (i, j))
    mu_full = mu.reshape(1, -1)
    in_specs = [pl.BlockSpec((tm, d), lambda j, i: (i, 0)),
                pl.BlockSpec((tm, 1), lambda j, i: (i, 0)),
                w_rows(0), w_rows(1), w_rows(2),
                pl.BlockSpec((tm, low_width), lambda j, i: (i, 0)),
                pl.BlockSpec((tm, tn), lambda j, i: (i, 2 * nc + j)),
                row(0), row(1), row(2), row(), row(), row(), row(), row(),
                lora(wup), lora(aup), lora(vup), lora(gup)]
    return pl.pallas_call(
        functools.partial(_rwkv_project_kernel, tiles_per_seq=seq_len // tm, segments=segments),
        out_shape=(jax.ShapeDtypeStruct((t, mix_w), f32),) * 7,
        grid=(nc, t // tm),
        in_specs=in_specs,
        out_specs=(out_spec,) * 7,
        scratch_shapes=[pltpu.VMEM((3 * tn, d), bf16), pltpu.VMEM((SUBLANES, 3 * tn), f32)],
        compiler_params=_params(("arbitrary", "arbitrary")),
        name="rwkv_project",
    )(hn, row_ss, w_t, w_t, w_t, low_acts, p_a, mu_full, mu_full, mu_full,
      w0.reshape(1, -1), a0.reshape(1, -1), v0.reshape(1, -1), k_k.reshape(1, -1),
      k_a.reshape(1, -1), wup, aup, vup, gup)


def _rwkv_scan_kernel(r_ref, lw_ref, k_ref, v_ref, kk_ref, b_ref, g_ref, lng_ref, lnb_ref, rk_ref,
                      o_ref, state_ref):
    @pl.when(pl.program_id(2) == 0)
    def _():
        state_ref[...] = jnp.zeros_like(state_ref)

    c = SCAN_CHUNK
    n = 2 * c
    pairs = range(r_ref.shape[1] // LANES)
    cols = [slice(p * LANES, (p + 1) * LANES) for p in pairs]

    tri_r = lax.broadcasted_iota(jnp.int32, (c, c), 0)
    tri_c = lax.broadcasted_iota(jnp.int32, (c, c), 1)
    tri = jnp.where(tri_r >= tri_c, 1.0, 0.0).astype(bf16)
    lw_all = lw_ref[...]
    hi = lw_all.astype(bf16)
    rest = lw_all - hi.astype(f32)
    mid = rest.astype(bf16)
    lo = (rest - mid.astype(f32)).astype(bf16)
    lcum_all = jnp.dot(jnp.concatenate([tri, tri, tri], axis=1),
                       jnp.concatenate([hi, mid, lo], axis=0), preferred_element_type=f32)

    head0 = lax.broadcasted_iota(jnp.int32, (c, LANES), 1) < R_HEAD_DIM
    row = lax.broadcasted_iota(jnp.int32, (n, n), 0)
    col = lax.broadcasted_iota(jnp.int32, (n, n), 1)
    rin, cin = row & (c - 1), col & (c - 1)
    strict, incl = rin > cin, rin >= cin
    eye = jnp.where(row == col, 1.0, 0.0)

    def merge_mask(level):
        rb, cb = row >> level, col >> level
        return ((rb >> 1) == (cb >> 1)) & ((rb & 1) == 1) & ((cb & 1) == 0)

    def stack(x):
        return jnp.concatenate([jnp.where(head0, x, 0.0), jnp.where(head0, 0.0, x)], axis=0)

    ops = []
    for sl in cols:
        lcum, lw = lcum_all[:, sl], lw_all[:, sl]
        r, k, v, kk, b = r_ref[:, sl], k_ref[:, sl], v_ref[:, sl], kk_ref[:, sl], b_ref[:, sl]
        l_end = lcum[c - 1:c, :]
        e_in = jnp.exp(lcum)
        e_out = jnp.exp(-lcum)
        e_end = jnp.exp(l_end - lcum)
        ops.append(dict(
            a_t=stack(-kk * jnp.exp(lcum - lw)), r_t=stack(r * e_in),
            b_t=stack(b * e_out), k_t=stack(k * e_out),
            bk_h=jnp.concatenate([stack(b * e_end), stack(k * e_end)], axis=0),
            v_s=stack(v), decay=jnp.exp(l_end)))

    for o in ops:
        s1 = _dot_nt(jnp.concatenate([o["a_t"], o["r_t"]], axis=0),
                     jnp.concatenate([o["b_t"], o["k_t"]], axis=0))
        o["ab"] = jnp.where(strict, s1[:n, :n], 0.0)
        o["ak"] = jnp.where(strict, s1[:n, n:], 0.0)
        o["rbk"] = jnp.concatenate([jnp.where(incl, s1[n:, :n], 0.0),
                                    jnp.where(incl, s1[n:, n:], 0.0)], axis=1)

    mask = merge_mask(0)
    ts = [eye + jnp.where(mask, o["ab"], 0.0) for o in ops]
    for level in range(1, int(math.log2(c))):
        mask = merge_mask(level)
        zs = [_dot(jnp.where(mask, o["ab"], 0.0), t) for o, t in zip(ops, ts)]
        ts = [t + _dot(t, z) for t, z in zip(ts, zs)]

    states = [state_ref[p] for p in pairs]
    ar_m0 = [_dot_nt(jnp.concatenate([o["a_t"], o["r_t"]], axis=0), s) for o, s in zip(ops, states)]
    rhs = [x[:n, :] + _dot(o["ak"], o["v_s"]) for x, o in zip(ar_m0, ops)]
    us = [_dot(t, x) for t, x in zip(ts, rhs)]
    uvs = [jnp.concatenate([u, o["v_s"]], axis=0) for u, o in zip(us, ops)]
    y_stacked = [x[n:, :] + _dot(o["rbk"], uv) for x, o, uv in zip(ar_m0, ops, uvs)]
    for p, o, s, uv in zip(pairs, ops, states, uvs):
        state_ref[p] = s * o["decay"] + _dot_tn(uv, o["bk_h"])

    ones = _head_block_ones(LANES, R_HEAD_DIM)
    inv_n = 1.0 / R_HEAD_DIM
    ys = [y[:c, :] + y[c:, :] for y in y_stacked]
    sums = [_split_dot(jnp.concatenate([y, r_ref[:, sl] * k_ref[:, sl] * rk_ref[:, sl]], axis=0), ones)
            for y, sl in zip(ys, cols)]
    mus = [x[:c, :] * inv_n for x in sums]
    bonus = [x[c:, :] for x in sums]
    ycs = [y - mu for y, mu in zip(ys, mus)]
    vars_ = [_split_dot(yc * yc, ones) * inv_n for yc in ycs]
    for sl, yc, var, bo in zip(cols, ycs, vars_, bonus):
        y = yc * lax.rsqrt(var + GN_EPS) * lng_ref[:, sl] + lnb_ref[:, sl]
        o_ref[:, sl] = ((y + bo * v_ref[:, sl]) * g_ref[:, sl]).astype(o_ref.dtype)


def rwkv_scan(r, lw, k, v, kk, b, g, ln_g, ln_b, r_k, batch, seq_len, *, wb=3072):
    t, mix_w = r.shape
    wb = _pick_tile(mix_w, wb)
    assert mix_w % wb == 0 and wb % LANES == 0 and seq_len % SCAN_CHUNK == 0
    nchunk = seq_len // SCAN_CHUNK
    tile = pl.BlockSpec((SCAN_CHUNK, wb), lambda bi, w, n: (bi * nchunk + n, w))
    row = pl.BlockSpec((1, wb), lambda bi, w, n: (0, w))
    return pl.pallas_call(
        _rwkv_scan_kernel,
        out_shape=jax.ShapeDtypeStruct((t, mix_w), bf16),
        grid=(batch, mix_w // wb, nchunk),
        in_specs=[tile] * 7 + [row] * 3,
        out_specs=tile,
        scratch_shapes=[pltpu.VMEM((wb // LANES, LANES, LANES), f32)],
        compiler_params=_params(("parallel", "parallel", "arbitrary")),
        name="rwkv_scan",
    )(r, lw, k, v, kk, b, g, ln_g.reshape(1, -1), ln_b.reshape(1, -1), r_k.reshape(1, -1))


def _round_up(n, multiple):
    return -(-n // multiple) * multiple


def _pack_segments(x, sizes, axis):
    parts, start = [], 0
    for s in sizes:
        seg = lax.slice_in_dim(x, start, start + s, axis=axis)
        pad = [(0, 0)] * x.ndim
        pad[axis] = (0, _round_up(s, LANES) - s)
        parts.append(jnp.pad(seg, pad))
        start += s
    return jnp.concatenate(parts, axis=axis)


def _pad_rows(w, total):
    return jnp.pad(w.astype(bf16), ((0, total - w.shape[0]), (0, 0)))


def kernel(x, mem, positions, mem_norm_g, mem_w_kv, a_norm_g, a_w_in, a_w_out, b_norm_g, b_w_in, b_shift_mu, b_w0, b_w_up, b_a0, b_a_up, b_v0, b_v_up, b_g_up, b_k_k, b_k_a, b_r_k, b_ln_g, b_ln_b, b_w_out, ffn_norm_g, ffn_w_up, ffn_conv_w, ffn_conv_b, ffn_w_down, final_norm_g):
    batch, seq_len, d_model = x.shape
    t = batch * seq_len
    depth = ffn_w_up.shape[0]
    xw = d_model // 4
    mix_w = d_model - xw
    x = x.reshape(t, d_model)

    mem_h = rmsnorm(mem.reshape(-1, d_model), mem_norm_g, bf16)
    mem_kv = matmul(mem_h, mem_w_kv[None], 0)
    tabs = rope_tables(positions)

    def mixer_gain(layer):
        return (a_norm_g if layer % 2 == 0 else b_norm_g)[layer // 2]

    p_a = None
    h, row_ss = rmsnorm(x, mixer_gain(0), bf16), None
    for i in range(depth):
        j = i // 2
        if i % 2 == 0:
            p = matmul(h, a_w_in, j, row_ss=row_ss, tm=IN_PROJ_ROWS)
            if p_a is None:
                p_a = p
            mix = dilated_attn(p, tabs, batch, seq_len)
            xo = cross_attn(p, 3 * mix_w // xw, mem_kv, batch, seq_len)
            w_out = a_w_out
        else:
            w_bt = jnp.swapaxes(b_w_in, 1, 2)
            ups = (b_w_up[j], b_a_up[j], b_v_up[j], b_g_up[j])
            sizes = [u.shape[0] for u in ups]
            low_w = sum(sizes)
            w_low_t = _pack_segments(w_bt[j, 3 * mix_w:3 * mix_w + low_w], sizes, 0)[None]
            mu_low = _pack_segments(b_shift_mu[j][None, 3 * mix_w:], sizes, 1)
            segments, start = [], 0
            for s in sizes:
                segments.append((start, start + _round_up(s, LANES)))
                start = segments[-1][1]
            p_low = matmul(h, w_low_t, 0, w_is_transposed=True, row_ss=row_ss, tn=start)
            low_acts = low_rank_acts(p_low, mu_low, seq_len, tuple(segments))
            cq = matmul(h, w_bt[j:j + 1, 3 * mix_w + low_w:], 0, w_is_transposed=True,
                        row_ss=row_ss, out_dtype=bf16)
            prep = rwkv_project(
                (h, row_ss), w_bt, j, low_acts, tuple(segments), p_a, mix_w, seq_len,
                b_shift_mu[j], b_w0[j], b_a0[j], b_v0[j], b_k_k[j], b_k_a[j],
                *[_pad_rows(u, _round_up(u.shape[0], LANES)) for u in ups])
            mix = rwkv_scan(*prep, b_ln_g[j], b_ln_b[j], b_r_k[j], batch, seq_len)
            xo = cross_attn(cq, 0, mem_kv, batch, seq_len)
            w_out = b_w_out
        x, normed = out_proj(mix, xo, w_out, j, x, ffn_norm_g[i])
        act = ffn_up(normed, ffn_w_up, ffn_conv_w, ffn_conv_b, i, seq_len)
        if i + 1 < depth:
            x, (h, row_ss) = matmul(act, ffn_w_down, i, res=x, norm_gain=mixer_gain(i + 1),
                                    tn=1024, tk=2048)
        else:
            x = matmul(act, ffn_w_down, i, res=x, tn=1024, tk=2048)
    return rmsnorm(x, final_norm_g, f32).reshape(batch, seq_len, d_model)
```
